```python
import math
import jax
import jax.numpy as jnp
from jax import lax
import numpy as np

D_MODEL = 2048
BATCH = 2
SEQ = 4096
DEPTH = 2
DEC_BATCH = 32
DEC_SEQ = 64
PAST_LEN = 2048

CHUNK = 64
Q_BLOCK = 128
D_MIX = D_MODEL
H_ATT = 8
HD_ATT = 128
DH_HALF = HD_ATT // 2
D_ATT = H_ATT * HD_ATT
H_GDN = 8
DK_GDN = 128
DV_GDN = 128
D_QK_GDN = H_GDN * DK_GDN
D_GDN = H_GDN * DV_GDN
CONV_W = 4
CONV_CH = 2 * D_QK_GDN + D_GDN
D_FF = 5632
NUM_BUCKETS = 32
RPB_MAX_DIST = 128
N_MOD = 9
N_IN = 3 * D_ATT + CONV_CH + D_GDN + 2 * H_GDN
EPS = 1e-6

kernel_name = 'hymba_diffattn_gdn_macaron_stream'


def rms_norm(x, g):
    xf = x.astype(jnp.float32)
    y = xf * lax.rsqrt(jnp.mean(xf * xf, axis=-1, keepdims=True) + EPS)
    return (y * g.astype(jnp.float32)).astype(x.dtype)


def l2_norm(x):
    xf = x.astype(jnp.float32)
    return (xf * lax.rsqrt(jnp.sum(xf * xf, axis=-1, keepdims=True) + EPS)).astype(x.dtype)


def swiglu(h, w_up, w_down):
    gate, up = jnp.split(h @ w_up, 2, axis=-1)
    return (jax.nn.silu(gate) * up) @ w_down


def t5_bucket(rel):
    half = NUM_BUCKETS // 2
    max_exact = half // 2
    n = jnp.abs(rel)
    large = max_exact + (jnp.log(jnp.maximum(n, 1).astype(jnp.float32) / max_exact)
                         / math.log(RPB_MAX_DIST / max_exact) * (half - max_exact)).astype(jnp.int32)
    large = jnp.minimum(large, half - 1)
    return jnp.where(rel > 0, half, 0) + jnp.where(n < max_exact, n, large)


def diff_attention(q, k, v, q_pos, k_pos, rpb_table, lam):
    B, Tq = q.shape[0], q.shape[1]
    k_chunk = k_pos // CHUNK

    def block(args):
        qb, pb = args
        logits = jnp.einsum('bqhcd,bkhcd->bhcqk', qb, k).astype(jnp.float32) * (DH_HALF ** -0.5)
        bias = jnp.transpose(rpb_table[t5_bucket(k_pos[None, :] - pb[:, None])], (2, 0, 1))
        logits = logits + bias[None, :, None].astype(jnp.float32)
        mask = k_chunk[None, :] <= (pb // CHUNK)[:, None]
        logits = jnp.where(mask[None, None, None], logits, -jnp.inf)
        p = jax.nn.softmax(logits, axis=-1)
        a = (p[:, :, 0] - lam * p[:, :, 1]).astype(v.dtype)
        return jnp.einsum('bhqk,bkhe->bqhe', a, v)

    if Tq > Q_BLOCK:
        nb = Tq // Q_BLOCK
        qb = jnp.moveaxis(q.reshape((B, nb, Q_BLOCK) + q.shape[2:]), 1, 0)
        out = lax.map(block, (qb, q_pos.reshape(nb, Q_BLOCK)))
        return jnp.moveaxis(out, 0, 1).reshape((B, Tq) + out.shape[3:])
    return block((q, q_pos))


def causal_conv(x, hist, w):
    T = x.shape[1]
    xp = jnp.concatenate([hist, x], axis=1)
    y = xp[:, 0:T] * w[0]
    for i in range(1, CONV_W):
        y = y + xp[:, i:i + T] * w[i]
    return jax.nn.silu(y), xp[:, xp.shape[1] - (CONV_W - 1):]


def gdn_chunked(q, k, v, g, beta, s0, chunk):
    B, T, H, DK = q.shape
    n = T // chunk

    def blk(t):
        t = t.astype(jnp.float32).reshape((B, n, chunk, H) + t.shape[3:])
        return jnp.moveaxis(t, 3, 1)

    q, k, v, g, beta = blk(q), blk(k), blk(v), blk(g), blk(beta)
    gc = jnp.cumsum(g, axis=-1)
    diff = gc[..., :, None] - gc[..., None, :]
    causal = jnp.tril(jnp.ones((chunk, chunk), bool))
    strict = jnp.tril(jnp.ones((chunk, chunk), bool), -1)
    gamma = jnp.where(causal, jnp.exp(jnp.where(causal, diff, 0.0)), 0.0)
    kb = k * beta[..., None]
    a = jnp.where(strict, jnp.einsum('bhncd,bhnsd->bhncs', kb, k) * gamma, 0.0)
    rhs = jnp.concatenate([kb * jnp.exp(gc)[..., None], v * beta[..., None]], axis=-1)
    sol = lax.linalg.triangular_solve(a + jnp.eye(chunk, dtype=jnp.float32), rhs,
                                      left_side=True, lower=True, unit_diagonal=True)
    w, u = sol[..., :DK], sol[..., DK:]
    attn = jnp.einsum('bhncd,bhnsd->bhncs', q, k) * gamma
    q_dec = q * jnp.exp(gc)[..., None]
    k_dec = k * jnp.exp(gc[..., -1:] - gc)[..., None]
    g_last = jnp.exp(gc[..., -1])

    def step(s, inp):
        w_n, u_n, attn_n, qd_n, kd_n, gl_n = inp
        v_new = u_n - jnp.einsum('bhcd,bhde->bhce', w_n, s)
        o = jnp.einsum('bhcd,bhde->bhce', qd_n, s) + jnp.einsum('bhcs,bhse->bhce', attn_n, v_new)
        s = s * gl_n[..., None, None] + jnp.einsum('bhcd,bhce->bhde', kd_n, v_new)
        return s, o

    xs = [jnp.moveaxis(t, 2, 0) for t in (w, u, attn, q_dec, k_dec, g_last)]
    s_fin, o = lax.scan(step, s0.astype(jnp.float32), xs)
    o = jnp.moveaxis(jnp.moveaxis(o, 0, 2), 1, 3).reshape(B, T, H, v.shape[-1])
    return o, s_fin


def mixer(h, p, k_hist, v_hist, conv_hist, s0, rpb_table, lam_init, gdn_chunk):
    B, T, _ = h.shape
    P = k_hist.shape[1]
    cuts = [D_ATT, 2 * D_ATT, 3 * D_ATT, 3 * D_ATT + CONV_CH,
            3 * D_ATT + CONV_CH + D_GDN, 3 * D_ATT + CONV_CH + D_GDN + H_GDN]
    qa, ka, va, qkv_g, z, a_g, b_g = jnp.split(h @ p['w_in'], cuts, axis=-1)

    qa = rms_norm(qa.reshape(B, T, H_ATT, 2, DH_HALF), p['q_norm'])
    ka = rms_norm(ka.reshape(B, T, H_ATT, 2, DH_HALF), p['k_norm'])
    k_new = ka.reshape(B, T, H_ATT, HD_ATT)
    v_new = va.reshape(B, T, H_ATT, HD_ATT)
    k_all = jnp.concatenate([k_hist, k_new], axis=1).reshape(B, P + T, H_ATT, 2, DH_HALF)
    v_all = jnp.concatenate([v_hist, v_new], axis=1)
    lq = p['lam'].astype(jnp.float32)
    lam = jnp.exp(jnp.sum(lq[0] * lq[1])) - jnp.exp(jnp.sum(lq[2] * lq[3])) + lam_init
    q_pos = P + jnp.arange(T, dtype=jnp.int32)
    k_pos = jnp.arange(P + T, dtype=jnp.int32)
    o_att = diff_attention(qa, k_all, v_all, q_pos, k_pos, rpb_table, lam)
    o_att = rms_norm(o_att, p['subln']) * (1.0 - lam_init)

    qkv_g, conv_new = causal_conv(qkv_g, conv_hist, p['conv_w'])
    qg, kg, vg = jnp.split(qkv_g, [D_QK_GDN, 2 * D_QK_GDN], axis=-1)
    qg = l2_norm(qg.reshape(B, T, H_GDN, DK_GDN)) * (DK_GDN ** -0.5)
    kg = l2_norm(kg.reshape(B, T, H_GDN, DK_GDN))
    vg = vg.reshape(B, T, H_GDN, DV_GDN)
    g = -jnp.exp(p['a_log'].astype(jnp.float32)) * jax.nn.softplus(
        a_g.astype(jnp.float32) + p['dt_bias'].astype(jnp.float32))
    beta = jax.nn.sigmoid(b_g.astype(jnp.float32))
    o_g, s_new = gdn_chunked(qg, kg, vg, g, beta, s0, gdn_chunk)
    o_g = rms_norm(o_g.astype(h.dtype), p['gdn_norm']) * jax.nn.silu(z.reshape(B, T, H_GDN, DV_GDN))

    y = jnp.concatenate([o_att.reshape(B, T, D_ATT), o_g.reshape(B, T, D_GDN)], axis=-1) @ p['w_out']
    return y, (k_new, v_new, s_new.astype(s0.dtype), conv_new)


def layer(x, c, p, k_hist, v_hist, conv_hist, s0, rpb_table, lam_init, gdn_chunk):
    B = x.shape[0]
    mod = (jax.nn.silu(c) @ p['w_mod'] + p['b_mod']).reshape(B, N_MOD, D_MODEL)

    def modulate(h, j):
        return rms_norm(h, p['norm_g'][j]) * (1.0 + mod[:, 3 * j + 1, None]) + mod[:, 3 * j, None]

    x = x + 0.5 * mod[:, 2, None] * swiglu(modulate(x, 0), p['ffn_up'][0], p['ffn_down'][0])
    y, new_state = mixer(modulate(x, 1), p, k_hist, v_hist, conv_hist, s0, rpb_table, lam_init, gdn_chunk)
    x = x + mod[:, 5, None] * y
    x = x + 0.5 * mod[:, 8, None] * swiglu(modulate(x, 2), p['ffn_up'][1], p['ffn_down'][1])
    return x, new_state


def setup_inputs(seed: int = 0) -> dict:
    key = jax.random.key(seed)
    ks = jax.random.split(key, 24)
    f32 = jnp.float32

    def nrm(k, shape, s):
        return s * jax.random.normal(k, shape, f32)

    dt = jnp.exp(jax.random.uniform(ks[20], (DEPTH, H_GDN), f32, math.log(1e-3), math.log(0.1)))
    return {
        'x_prompt': nrm(ks[0], (BATCH, SEQ, D_MODEL), 1.0),
        'x_sample': nrm(ks[1], (DEC_BATCH, DEC_SEQ, D_MODEL), 1.0),
        'cache_k': nrm(ks[2], (DEPTH, DEC_BATCH, PAST_LEN, H_ATT, HD_ATT), 1.0),
        'cache_v': nrm(ks[3], (DEPTH, DEC_BATCH, PAST_LEN, H_ATT, HD_ATT), 1.0),
        'state_gdn': nrm(ks[4], (DEPTH, DEC_BATCH, H_GDN, DK_GDN, DV_GDN), 0.1),
        'state_conv': nrm(ks[5], (DEPTH, DEC_BATCH, CONV_W - 1, CONV_CH), 1.0),
        'c_prompt': nrm(ks[6], (BATCH, D_MODEL), 1.0),
        'c_sample': nrm(ks[7], (DEC_BATCH, D_MODEL), 1.0),
        'w_mod': nrm(ks[8], (DEPTH, D_MODEL, N_MOD * D_MODEL), D_MODEL ** -0.5),
        'b_mod': nrm(ks[9], (DEPTH, N_MOD * D_MODEL), 0.02),
        'norm_g': 1.0 + nrm(ks[10], (DEPTH, 3, D_MODEL), 0.05),
        'ffn_up': nrm(ks[11], (DEPTH, 2, D_MODEL, 2 * D_FF), D_MODEL ** -0.5),
        'ffn_down': nrm(ks[12], (DEPTH, 2, D_FF, D_MODEL), D_FF ** -0.5),
        'w_in': nrm(ks[13], (DEPTH, D_MODEL, N_IN), D_MODEL ** -0.5),
        'w_out': nrm(ks[14], (DEPTH, D_MIX, D_MODEL), D_MIX ** -0.5),
        'qk_norm': 1.0 + nrm(ks[15], (DEPTH, 2, DH_HALF), 0.05),
        'lam_qk': nrm(ks[16], (DEPTH, 4, DH_HALF), 0.1),
        'subln_w': 1.0 + nrm(ks[17], (DEPTH, HD_ATT), 0.05),
        'rpb_table': nrm(ks[18], (NUM_BUCKETS, H_ATT), 0.5),
        'conv_w': nrm(ks[19], (DEPTH, CONV_W, CONV_CH), CONV_W ** -0.5),
        'a_log': jnp.log(jax.random.uniform(ks[21], (DEPTH, H_GDN), f32, 1.0, 16.0)),
        'dt_bias': dt + jnp.log(-jnp.expm1(-dt)),
        'gdn_norm_w': 1.0 + nrm(ks[22], (DEPTH, DV_GDN), 0.05),
    }


def reference(x_prompt, x_sample, cache_k, cache_v, state_gdn, state_conv, c_prompt, c_sample,
              w_mod, b_mod, norm_g, ffn_up, ffn_down, w_in, w_out, qk_norm, lam_qk, subln_w,
              rpb_table, conv_w, a_log, dt_bias, gdn_norm_w):
    B = x_prompt.shape[0]
    dt = x_prompt.dtype
    empty_kv = jnp.zeros((B, 0, H_ATT, HD_ATT), dt)
    zero_conv = jnp.zeros((B, CONV_W - 1, CONV_CH), dt)
    zero_s = jnp.zeros((B, H_GDN, DK_GDN, DV_GDN), dt)
    xp, xs = x_prompt, x_sample
    k_p, v_p, s_p, c_p, k_s, v_s, s_s, c_s = [], [], [], [], [], [], [], []
    for l in range(DEPTH):
        p = {'w_mod': w_mod[l], 'b_mod': b_mod[l], 'norm_g': norm_g[l],
             'ffn_up': ffn_up[l], 'ffn_down': ffn_down[l], 'w_in': w_in[l], 'w_out': w_out[l],
             'q_norm': qk_norm[l, 0], 'k_norm': qk_norm[l, 1], 'lam': lam_qk[l], 'subln': subln_w[l],
             'conv_w': conv_w[l], 'a_log': a_log[l], 'dt_bias': dt_bias[l], 'gdn_norm': gdn_norm_w[l]}
        lam_init = 0.8 - 0.6 * math.exp(-0.3 * l)
        xp, (k1, v1, s1, c1) = layer(xp, c_prompt, p, empty_kv, empty_kv, zero_conv, zero_s,
                                     rpb_table, lam_init, CHUNK)
        xs, (k2, v2, s2, c2) = layer(xs, c_sample, p, cache_k[l], cache_v[l], state_conv[l], state_gdn[l],
                                     rpb_table, lam_init, xs.shape[1])
        k_p.append(k1)
        v_p.append(v1)
        s_p.append(s1)
        c_p.append(c1)
        k_s.append(k2)
        v_s.append(v2)
        s_s.append(s2)
        c_s.append(c2)
    return (xp, xs, jnp.stack(k_p), jnp.stack(v_p), jnp.stack(s_p), jnp.stack(c_p),
            jnp.stack(k_s), jnp.stack(v_s), jnp.stack(s_s), jnp.stack(c_s))
```

```python
import functools
import math

import jax
import jax.numpy as jnp
from jax import lax
from jax.experimental import pallas as pl
from jax.experimental.pallas import tpu as pltpu

f32 = jnp.float32
bf16 = jnp.bfloat16

D = 2048
BATCH = 2
SEQ = 4096
DEPTH = 2
DEC_BATCH = 32
DEC_SEQ = 64
PAST = 2048
CHUNK = 64
H = 8
HD = 128
HALF = 64
D_ATT = H * HD
D_GDN = H * HD
CONV_W = 4
CONV_CH = 3 * D_GDN
D_FF = 5632
NUM_BUCKETS = 32
N_MOD = 9
N_IN = 3 * D_ATT + CONV_CH + D_GDN + 2 * H
EPS = 1e-6

GROUP = 64
P_TOK = BATCH * SEQ
S_TOK = DEC_BATCH * DEC_SEQ
TOKENS = P_TOK + S_TOK
N_GROUPS = TOKENS // GROUP
N_CBATCH = 40
N_PROJ = 7680
COL_K, COL_V, COL_CONV, COL_Z, COL_AB = D_ATT, 2 * D_ATT, 3 * D_ATT, 3 * D_ATT + CONV_CH, 3 * D_ATT + CONV_CH + D_GDN

V7X_VMEM_BYTES = 64 * 1024 * 1024
MASKED = -1e30

FFN_TM, FFN_FC = 512, 512
PROJ_TM, PROJ_TN = 1024, 512
OUT_TM = 512
MOD_TN = 1024
ATT_TQ = 256
ATT_TK = 256
SATT_TK = 512


def _params(sem, vmem_mb):
    return pltpu.CompilerParams(dimension_semantics=sem, vmem_limit_bytes=vmem_mb * 1024 * 1024)


def _dot(a, b):
    return jnp.dot(a, b, preferred_element_type=f32)


def _dot_nt(a, b):
    return lax.dot_general(a, b, (((1,), (1,)), ((), ())), preferred_element_type=f32)


def _dot_exact(a, b):
    return jnp.dot(a, b, preferred_element_type=f32, precision=lax.Precision.HIGHEST)


def _silu(x):
    return x * jax.nn.sigmoid(x)


def _modulated_norm(x, ng, scale, shift):
    tm = x.shape[0]
    ms = jnp.mean(x * x, axis=-1, keepdims=True)
    y = x * lax.rsqrt(ms + EPS) * ng
    y = y.reshape(tm // GROUP, GROUP, D) * (1.0 + scale[:, None, :]) + shift[:, None, :]
    return y.reshape(tm, D)


def _gated(y, gate):
    tm = y.shape[0]
    return (y.reshape(tm // GROUP, GROUP, D) * gate[:, None, :]).reshape(tm, D)


def _blockdiag_half():
    r = lax.broadcasted_iota(jnp.int32, (HD, HD), 0) // HALF
    c = lax.broadcasted_iota(jnp.int32, (HD, HD), 1) // HALF
    return (r == c).astype(bf16)


def _half_rms(x, w, bd):
    ss = _dot((x * x).astype(bf16), bd) * (1.0 / HALF)
    return x * lax.rsqrt(ss + EPS) * w


def _mod_kernel(c_ref, w_ref, b_ref, o_ref):
    a = _silu(c_ref[...]).astype(bf16)
    o_ref[...] = _dot(a, w_ref[...].astype(bf16)) + b_ref[...]


def _mod_call(c_all, w_mod, b_mod):
    n = N_MOD * D // MOD_TN
    return pl.pallas_call(
        _mod_kernel,
        grid=(DEPTH, n),
        in_specs=[
            pl.BlockSpec((N_CBATCH, D), lambda l, j: (0, 0)),
            pl.BlockSpec((None, D, MOD_TN), lambda l, j: (l, 0, j)),
            pl.BlockSpec((None, 1, MOD_TN), lambda l, j: (l, 0, j)),
        ],
        out_specs=pl.BlockSpec((None, N_CBATCH, MOD_TN), lambda l, j: (l, 0, j)),
        out_shape=jax.ShapeDtypeStruct((DEPTH, N_CBATCH, N_MOD * D), f32),
        compiler_params=_params(("arbitrary", "arbitrary"), 40),
        name="adaln_mod",
    )(c_all, w_mod, b_mod.reshape(DEPTH, 1, N_MOD * D))


def _ffn_kernel(x_ref, shift_ref, scale_ref, gate_ref, ng_ref, wg_ref, wu_ref, wd_ref, o_ref, h_scr, acc_scr):
    k = pl.program_id(1)

    @pl.when(k == 0)
    def _():
        h_scr[...] = _modulated_norm(x_ref[...], ng_ref[...], scale_ref[...], shift_ref[...]).astype(bf16)
        acc_scr[...] = jnp.zeros_like(acc_scr)

    h = h_scr[...]
    g = _dot(h, wg_ref[...])
    u = _dot(h, wu_ref[...])
    acc_scr[...] += _dot((_silu(g) * u).astype(bf16), wd_ref[...])

    @pl.when(k == pl.num_programs(1) - 1)
    def _():
        o_ref[...] = x_ref[...] + 0.5 * _gated(acc_scr[...], gate_ref[...])


def _ffn_call(x, modg, norm_g, w_up, w_down, layer, idx, j):
    nk = D_FF // FFN_FC
    gm = FFN_TM // GROUP
    row = lambda r: pl.BlockSpec((None, gm, D), lambda m, k: (r, m, 0))
    return pl.pallas_call(
        _ffn_kernel,
        grid=(TOKENS // FFN_TM, nk),
        in_specs=[
            pl.BlockSpec((FFN_TM, D), lambda m, k: (m, 0)),
            row(3 * j), row(3 * j + 1), row(3 * j + 2),
            pl.BlockSpec((None, None, 1, D), lambda m, k: (layer, j, 0, 0)),
            pl.BlockSpec((None, None, D, FFN_FC), lambda m, k: (layer, idx, 0, k)),
            pl.BlockSpec((None, None, D, FFN_FC), lambda m, k: (layer, idx, 0, nk + k)),
            pl.BlockSpec((None, None, FFN_FC, D), lambda m, k: (layer, idx, k, 0)),
        ],
        out_specs=pl.BlockSpec((FFN_TM, D), lambda m, k: (m, 0)),
        out_shape=jax.ShapeDtypeStruct((TOKENS, D), f32),
        scratch_shapes=[pltpu.VMEM((FFN_TM, D), bf16), pltpu.VMEM((FFN_TM, D), f32)],
        compiler_params=_params(("arbitrary", "arbitrary"), 48),
        name="swiglu_half_step",
    )(x, modg, modg, modg, norm_g, w_up, w_up, w_down)


def _proj_kernel(x_ref, shift_ref, scale_ref, ng_ref, w_ref, o_ref, h_scr):
    @pl.when(pl.program_id(1) == 0)
    def _():
        h_scr[...] = _modulated_norm(x_ref[...], ng_ref[...], scale_ref[...], shift_ref[...]).astype(bf16)

    o_ref[...] = _dot(h_scr[...], w_ref[...])


def _proj_call(x, modg, norm_g, w_in, layer):
    gm = PROJ_TM // GROUP
    row = lambda r: pl.BlockSpec((None, gm, D), lambda m, n: (r, m, 0))
    return pl.pallas_call(
        _proj_kernel,
        grid=(TOKENS // PROJ_TM, N_PROJ // PROJ_TN),
        in_specs=[
            pl.BlockSpec((PROJ_TM, D), lambda m, n: (m, 0)),
            row(3), row(4),
            pl.BlockSpec((None, None, 1, D), lambda m, n: (layer, 1, 0, 0)),
            pl.BlockSpec((None, D, PROJ_TN), lambda m, n: (layer, 0, n)),
        ],
        out_specs=pl.BlockSpec((PROJ_TM, PROJ_TN), lambda m, n: (m, n)),
        out_shape=jax.ShapeDtypeStruct((TOKENS, N_PROJ), f32),
        scratch_shapes=[pltpu.VMEM((PROJ_TM, D), bf16)],
        compiler_params=_params(("arbitrary", "arbitrary"), 40),
        name="mixer_in_proj",
    )(x, modg, modg, norm_g, w_in)


def _out_kernel(x_ref, gate_ref, a_ref, g_ref, w_ref, o_ref):
    y = _dot(a_ref[...], w_ref[:D_ATT, :]) + _dot(g_ref[...], w_ref[D_ATT:, :])
    o_ref[...] = x_ref[...] + _gated(y, gate_ref[...])


def _out_call(x, modg, o_att, o_gdn, w_out, layer):
    gm = OUT_TM // GROUP
    return pl.pallas_call(
        _out_kernel,
        grid=(TOKENS // OUT_TM,),
        in_specs=[
            pl.BlockSpec((OUT_TM, D), lambda m: (m, 0)),
            pl.BlockSpec((None, gm, D), lambda m: (5, m, 0)),
            pl.BlockSpec((OUT_TM, D_ATT), lambda m: (m, 0)),
            pl.BlockSpec((OUT_TM, D_GDN), lambda m: (m, 0)),
            pl.BlockSpec((None, D, D), lambda m: (layer, 0, 0)),
        ],
        out_specs=pl.BlockSpec((OUT_TM, D), lambda m: (m, 0)),
        out_shape=jax.ShapeDtypeStruct((TOKENS, D), f32),
        compiler_params=_params(("arbitrary",), 48),
        name="mixer_out_proj",
    )(x, modg, o_att, o_gdn, w_out)


def _bias_from_rel(rel, table_ref, h):
    n = jnp.abs(rel)
    big = jnp.full_like(n, 8)
    for t in (12, 16, 23, 32, 46, 64, 91):
        big = big + (n >= t).astype(jnp.int32)
    bucket = jnp.where(n < 8, n, big) + jnp.where(rel > 0, NUM_BUCKETS // 2, 0)
    out = jnp.zeros(rel.shape, f32)
    for b in range(NUM_BUCKETS):
        out = jnp.where(bucket == b, table_ref[b, h], out)
    return out


def _bias_kernel(table_ref, pb_ref, sb_ref):
    h = pl.program_id(0)
    qi = lax.broadcasted_iota(jnp.int32, (ATT_TQ, ATT_TK), 0)
    kj = lax.broadcasted_iota(jnp.int32, (ATT_TQ, ATT_TK), 1)
    for d in range(3):
        tile = _bias_from_rel(kj - qi - d * ATT_TQ, table_ref, h)
        if d == 0:
            tile = jnp.where(kj // CHUNK <= qi // CHUNK, tile, MASKED)
        pb_ref[d, :ATT_TQ, :] = tile
        pb_ref[d, ATT_TQ:, :] = tile
    n_keys = PAST + HD
    sq = lax.broadcasted_iota(jnp.int32, (DEC_SEQ, n_keys), 0)
    sk = lax.broadcasted_iota(jnp.int32, (DEC_SEQ, n_keys), 1)
    tile = _bias_from_rel(sk - (PAST + sq), table_ref, h)
    tile = jnp.where(sk < PAST + DEC_SEQ, tile, MASKED)
    sb_ref[:DEC_SEQ, :] = tile
    sb_ref[DEC_SEQ:, :] = tile


def _bias_call(rpb_table):
    return pl.pallas_call(
        _bias_kernel,
        grid=(H,),
        in_specs=[pl.BlockSpec(memory_space=pltpu.SMEM)],
        out_specs=[
            pl.BlockSpec((None, 3, 2 * ATT_TQ, ATT_TK), lambda h: (h, 0, 0, 0)),
            pl.BlockSpec((None, 2 * DEC_SEQ, PAST + HD), lambda h: (h, 0, 0)),
        ],
        out_shape=[
            jax.ShapeDtypeStruct((H, 3, 2 * ATT_TQ, ATT_TK), f32),
            jax.ShapeDtypeStruct((H, 2 * DEC_SEQ, PAST + HD), f32),
        ],
        compiler_params=_params(("arbitrary",), 32),
        name="rel_pos_bias_tiles",
    )(rpb_table)


def _two_map_queries(qn):
    lane = lax.broadcasted_iota(jnp.int32, qn.shape, 1)
    return jnp.concatenate([jnp.where(lane < HALF, qn, 0.0), jnp.where(lane >= HALF, qn, 0.0)], axis=0).astype(bf16)


def _softmax_step(s, v, m, l, acc):
    m_new = jnp.maximum(m, jnp.max(s, axis=-1, keepdims=True))
    p = jnp.exp(s - m_new)
    alpha = jnp.exp(m - m_new)
    l = alpha * l + jnp.sum(p, axis=-1, keepdims=True)
    acc = alpha * acc + _dot(p.astype(bf16), v)
    return m_new, l, acc


def _diff_combine(l, acc, lam_ref, sw, lam_init):
    n = acc.shape[0] // 2
    lq = lam_ref[...]
    lam = (jnp.exp(jnp.sum(lq[0:1] * lq[1:2], axis=-1, keepdims=True))
           - jnp.exp(jnp.sum(lq[2:3] * lq[3:4], axis=-1, keepdims=True)) + lam_init)
    o = acc[:n] / l[:n] - lam * (acc[n:] / l[n:])
    o = o * lax.rsqrt(jnp.mean(o * o, axis=-1, keepdims=True) + EPS) * sw
    return o * (1.0 - lam_init)


def _attn_prompt_kernel(lam_ref, qw_ref, kw_ref, sw_ref, bias_ref, q_ref, k_ref, v_ref,
                        o_ref, knew_ref, kn_scr, vb_scr, *, lam_init):
    qi = pl.program_id(2)
    bd = _blockdiag_half()

    @pl.when(qi == 0)
    def _():
        kn = _half_rms(k_ref[...], kw_ref[...], bd)
        knew_ref[...] = kn
        kn_scr[...] = kn.astype(bf16)
        vb_scr[...] = v_ref[...].astype(bf16)

    q2 = _two_map_queries(_half_rms(q_ref[...], qw_ref[...], bd) * (HALF ** -0.5))

    def body(j, carry):
        rows = pl.ds(pl.multiple_of(j * ATT_TK, ATT_TK), ATT_TK)
        s = _dot_nt(q2, kn_scr[rows, :]) + bias_ref[jnp.minimum(qi - j, 2)]
        return _softmax_step(s, vb_scr[rows, :], *carry)

    init = (jnp.full((2 * ATT_TQ, 1), MASKED, f32), jnp.zeros((2 * ATT_TQ, 1), f32), jnp.zeros((2 * ATT_TQ, HD), f32))
    _, l, acc = lax.fori_loop(0, qi + 1, body, init)
    o_ref[...] = _diff_combine(l, acc, lam_ref, sw_ref[...], lam_init).astype(bf16)


def _attn_prompt_call(p, bias, lam_qk, qk_norm2, subln_w, layer, lam_init):
    nq = SEQ // ATT_TQ
    vec = lambda r: pl.BlockSpec((None, None, 1, HD), lambda b, h, i: (layer, r, 0, 0))
    return pl.pallas_call(
        functools.partial(_attn_prompt_kernel, lam_init=lam_init),
        grid=(BATCH, H, nq),
        in_specs=[
            pl.BlockSpec((None, 4, HALF), lambda b, h, i: (layer, 0, 0)),
            vec(0), vec(1),
            pl.BlockSpec((None, 1, HD), lambda b, h, i: (layer, 0, 0)),
            pl.BlockSpec((None, 3, 2 * ATT_TQ, ATT_TK), lambda b, h, i: (h, 0, 0, 0)),
            pl.BlockSpec((ATT_TQ, HD), lambda b, h, i: (b * nq + i, h)),
            pl.BlockSpec((SEQ, HD), lambda b, h, i: (b, H + h)),
            pl.BlockSpec((SEQ, HD), lambda b, h, i: (b, 2 * H + h)),
        ],
        out_specs=[
            pl.BlockSpec((ATT_TQ, HD), lambda b, h, i: (b * nq + i, h)),
            pl.BlockSpec((SEQ, HD), lambda b, h, i: (b, h)),
        ],
        out_shape=[
            jax.ShapeDtypeStruct((P_TOK, D_ATT), bf16),
            jax.ShapeDtypeStruct((P_TOK, D_ATT), f32),
        ],
        scratch_shapes=[pltpu.VMEM((SEQ, HD), bf16), pltpu.VMEM((SEQ, HD), bf16)],
        compiler_params=_params(("arbitrary", "arbitrary", "arbitrary"), 40),
        name="diff_attn_prompt",
    )(lam_qk, qk_norm2, qk_norm2, subln_w.reshape(DEPTH, 1, HD), bias, p, p, p)


def _attn_sample_kernel(lam_ref, qw_ref, kw_ref, sw_ref, bias_ref, bnew_ref, q_ref, kx_ref, vx_ref, ck_ref, cv_ref,
                        o_ref, knew_ref, q2_scr, kn_scr, m_scr, l_scr, acc_scr, *, lam_init):
    ks = pl.program_id(1)
    last = pl.num_programs(1) - 1
    bd = _blockdiag_half()

    @pl.when(ks == 0)
    def _():
        m_scr[...] = jnp.full_like(m_scr, MASKED)
        l_scr[...] = jnp.zeros_like(l_scr)
        acc_scr[...] = jnp.zeros_like(acc_scr)
        for h in range(H):
            cols = slice(h * HD, (h + 1) * HD)
            q2_scr[h] = _two_map_queries(_half_rms(q_ref[:, cols], qw_ref[...], bd) * (HALF ** -0.5))
            kn = _half_rms(kx_ref[:, cols], kw_ref[...], bd)
            knew_ref[:, cols] = kn
            kn_scr[:, cols] = kn.astype(bf16)

    kb = ck_ref[...].astype(bf16)
    vb = cv_ref[...].astype(bf16)
    for h in range(H):
        cols = slice(h * HD, (h + 1) * HD)
        s = _dot_nt(q2_scr[h], kb[:, cols]) + bias_ref[h]
        m_scr[h], l_scr[h], acc_scr[h] = _softmax_step(s, vb[:, cols], m_scr[h], l_scr[h], acc_scr[h])

    @pl.when(ks == last)
    def _():
        vnew = vx_ref[...].astype(bf16)
        for h in range(H):
            cols = slice(h * HD, (h + 1) * HD)
            s = _dot_nt(q2_scr[h], kn_scr[:, cols]) + bnew_ref[h][:, :DEC_SEQ]
            _, l, acc = _softmax_step(s, vnew[:, cols], m_scr[h], l_scr[h], acc_scr[h])
            o_ref[:, cols] = _diff_combine(l, acc, lam_ref, sw_ref[...], lam_init).astype(bf16)


def _attn_sample_call(p, bias, cache_k, cache_v, lam_qk, qk_norm2, subln_w, layer, lam_init):
    nks = PAST // SATT_TK
    row0 = P_TOK // DEC_SEQ
    vec = lambda r: pl.BlockSpec((None, None, 1, HD), lambda b, s: (layer, r, 0, 0))
    sec = lambda c: pl.BlockSpec((DEC_SEQ, D_ATT), lambda b, s: (row0 + b, c))
    cache = pl.BlockSpec((None, None, SATT_TK, D_ATT), lambda b, s: (layer, b, s, 0))
    return pl.pallas_call(
        functools.partial(_attn_sample_kernel, lam_init=lam_init),
        grid=(DEC_BATCH, nks),
        in_specs=[
            pl.BlockSpec((None, 4, HALF), lambda b, s: (layer, 0, 0)),
            vec(0), vec(1),
            pl.BlockSpec((None, 1, HD), lambda b, s: (layer, 0, 0)),
            pl.BlockSpec((H, 2 * DEC_SEQ, SATT_TK), lambda b, s: (0, 0, s)),
            pl.BlockSpec((H, 2 * DEC_SEQ, HD), lambda b, s: (0, 0, PAST // HD)),
            sec(0), sec(1), sec(2), cache, cache,
        ],
        out_specs=[
            pl.BlockSpec((DEC_SEQ, D_ATT), lambda b, s: (b, 0)),
            pl.BlockSpec((DEC_SEQ, D_ATT), lambda b, s: (b, 0)),
        ],
        out_shape=[
            jax.ShapeDtypeStruct((S_TOK, D_ATT), bf16),
            jax.ShapeDtypeStruct((S_TOK, D_ATT), f32),
        ],
        scratch_shapes=[
            pltpu.VMEM((H, 2 * DEC_SEQ, HD), bf16),
            pltpu.VMEM((DEC_SEQ, D_ATT), bf16),
            pltpu.VMEM((H, 2 * DEC_SEQ, 1), f32),
            pltpu.VMEM((H, 2 * DEC_SEQ, 1), f32),
            pltpu.VMEM((H, 2 * DEC_SEQ, HD), f32),
        ],
        compiler_params=_params(("arbitrary", "arbitrary"), 48),
        name="diff_attn_sample",
    )(lam_qk, qk_norm2, qk_norm2, subln_w.reshape(DEPTH, 1, HD), bias, bias, p, p, p, cache_k, cache_v)


def _unit_lower_inverse(a):
    r = lax.broadcasted_iota(jnp.int32, a.shape, 0)
    c = lax.broadcasted_iota(jnp.int32, a.shape, 1)
    x = jnp.where(r == c, 1.0, 0.0) - a
    pw = a
    for _ in range(int(math.log2(CHUNK)) - 1):
        pw = _dot_exact(pw, pw)
        x = x + _dot_exact(x, pw)
    return x


def _gdn_kernel(p_ref, z_ref, ab_ref, hist_ref, s0_ref, cw_ref, alog_ref, dtb_ref, gw_ref,
                o_ref, sfin_ref, cnew_ref, xp_scr, s_scr):
    c = pl.program_id(1)

    @pl.when(c == 0)
    def _():
        xp_scr[0:8, :] = hist_ref[...]
        s_scr[...] = s0_ref[...]

    xp_scr[8:8 + CHUNK, :] = p_ref[...]
    y = xp_scr[8 - 3:8 - 3 + CHUNK, :] * cw_ref[0:1, :]
    for i in range(1, CONV_W):
        y = y + xp_scr[8 - 3 + i:8 - 3 + i + CHUNK, :] * cw_ref[i:i + 1, :]
    y = _silu(y)
    tail = xp_scr[CHUNK:CHUNK + 8, :]
    xp_scr[0:8, :] = tail
    cnew_ref[...] = tail

    ab = ab_ref[...]
    pre = ab + dtb_ref[...]
    softplus = jnp.maximum(pre, 0.0) + jnp.log1p(jnp.exp(-jnp.abs(pre)))
    g_all = -jnp.exp(alog_ref[...]) * softplus
    beta_all = jax.nn.sigmoid(ab)

    ii = lax.broadcasted_iota(jnp.int32, (CHUNK, CHUNK), 0)
    jj = lax.broadcasted_iota(jnp.int32, (CHUNK, CHUNK), 1)
    causal = ii >= jj
    strict = ii > jj

    for h in range(H):
        cols = slice(h * HD, (h + 1) * HD)
        qx = y[:, cols]
        kx = y[:, D_GDN + h * HD:D_GDN + (h + 1) * HD]
        v = y[:, 2 * D_GDN + h * HD:2 * D_GDN + (h + 1) * HD]
        q = qx * lax.rsqrt(jnp.sum(qx * qx, axis=-1, keepdims=True) + EPS) * (HD ** -0.5)
        k = kx * lax.rsqrt(jnp.sum(kx * kx, axis=-1, keepdims=True) + EPS)
        g = g_all[:, h:h + 1]
        beta = beta_all[:, H + h:H + h + 1]

        gc_row = jnp.sum(jnp.where(ii <= jj, g, 0.0), axis=0, keepdims=True)
        gc = jnp.sum(jnp.where(ii == jj, gc_row, 0.0), axis=1, keepdims=True)
        gc_last = jnp.sum(g, axis=0, keepdims=True)
        gamma = jnp.where(causal, jnp.exp(jnp.where(causal, gc - gc_row, 0.0)), 0.0)
        egc = jnp.exp(gc)

        kb = k * beta
        k16 = k.astype(bf16)
        a = jnp.where(strict, _dot_nt(kb.astype(bf16), k16) * gamma, 0.0)
        t = _unit_lower_inverse(a)
        sol = _dot_exact(t, jnp.concatenate([kb * egc, v * beta], axis=-1))
        w = sol[:, :HD]
        u = sol[:, HD:]
        attn = _dot_nt(q.astype(bf16), k16) * gamma
        q_dec = q * egc
        k_dec = k * jnp.exp(gc_last - gc)

        s = s_scr[h]
        s16 = s.astype(bf16)
        v_new = u - _dot(w.astype(bf16), s16)
        vn16 = v_new.astype(bf16)
        o = _dot(q_dec.astype(bf16), s16) + _dot(attn.astype(bf16), vn16)
        s_scr[h] = s * jnp.exp(gc_last) + _dot(k_dec.T.astype(bf16), vn16)

        o = o * lax.rsqrt(jnp.mean(o * o, axis=-1, keepdims=True) + EPS) * gw_ref[...]
        o_ref[:, cols] = (o * _silu(z_ref[:, cols])).astype(bf16)

    @pl.when(c == pl.num_programs(1) - 1)
    def _():
        sfin_ref[...] = s_scr[...]


def _gdn_call(p, hist, s0, conv_w, a_log, dt_bias, gdn_norm_w, layer, n_seq, n_chunks, row0):
    rows = lambda col, width: pl.BlockSpec((CHUNK, width), lambda b, c: (row0 + b * n_chunks + c, col))
    vec = pl.BlockSpec((None, 1, HD), lambda b, c: (layer, 0, 0))
    return pl.pallas_call(
        _gdn_kernel,
        grid=(n_seq, n_chunks),
        in_specs=[
            rows(COL_CONV // CONV_CH, CONV_CH),
            rows(COL_Z // D_GDN, D_GDN),
            rows(COL_AB // HD, HD),
            pl.BlockSpec((None, 8, CONV_CH), lambda b, c: (b, 0, 0)),
            pl.BlockSpec((None, H, HD, HD), lambda b, c: (b, 0, 0, 0)),
            pl.BlockSpec((None, CONV_W, CONV_CH), lambda b, c: (layer, 0, 0)),
            vec, vec, vec,
        ],
        out_specs=[
            pl.BlockSpec((CHUNK, D_GDN), lambda b, c: (b * n_chunks + c, 0)),
            pl.BlockSpec((None, H, HD, HD), lambda b, c: (b, 0, 0, 0)),
            pl.BlockSpec((None, 8, CONV_CH), lambda b, c: (b, 0, 0)),
        ],
        out_shape=[
            jax.ShapeDtypeStruct((n_seq * n_chunks * CHUNK, D_GDN), bf16),
            jax.ShapeDtypeStruct((n_seq, H, HD, HD), f32),
            jax.ShapeDtypeStruct((n_seq, 8, CONV_CH), f32),
        ],
        scratch_shapes=[pltpu.VMEM((CHUNK + 8, CONV_CH), f32), pltpu.VMEM((H, HD, HD), f32)],
        compiler_params=_params(("arbitrary", "arbitrary"), 40),
        name="gated_deltanet",
    )(p, p, p, hist, s0, conv_w, a_log, dt_bias, gdn_norm_w)


def _pad_lanes(v):
    return jnp.pad(v, ((0, 0), (0, HD - v.shape[-1]))).reshape(DEPTH, 1, HD)


def kernel(x_prompt, x_sample, cache_k, cache_v, state_gdn, state_conv, c_prompt, c_sample, w_mod, b_mod, norm_g, ffn_up, ffn_down, w_in, w_out, qk_norm, lam_qk, subln_w, rpb_table, conv_w, a_log, dt_bias, gdn_norm_w):
    x = jnp.concatenate([x_prompt.reshape(P_TOK, D), x_sample.reshape(S_TOK, D)], axis=0)
    c_all = jnp.concatenate([c_prompt, c_sample, jnp.zeros((N_CBATCH - BATCH - DEC_BATCH, D), f32)], axis=0)

    w_up16 = ffn_up.astype(bf16)
    w_down16 = ffn_down.astype(bf16)
    w_in16 = jnp.pad(w_in.astype(bf16), ((0, 0), (0, 0), (0, N_PROJ - N_IN)))
    w_out16 = w_out.astype(bf16)
    norm_g4 = norm_g.reshape(DEPTH, 3, 1, D)
    qk_norm2 = jnp.concatenate([qk_norm, qk_norm], axis=-1).reshape(DEPTH, 2, 1, HD)
    gdn_w = gdn_norm_w.reshape(DEPTH, 1, HD)
    a_log_p, dt_bias_p = _pad_lanes(a_log), _pad_lanes(dt_bias)
    ck = cache_k.reshape(DEPTH, DEC_BATCH, PAST, D_ATT)
    cv = cache_v.reshape(DEPTH, DEC_BATCH, PAST, D_ATT)
    hist_s = jnp.pad(state_conv, ((0, 0), (0, 0), (8 - (CONV_W - 1), 0), (0, 0)))
    hist_p = jnp.zeros((BATCH, 8, CONV_CH), f32)
    s0_p = jnp.zeros((BATCH, H, HD, HD), f32)

    group_row = jnp.concatenate([jnp.repeat(jnp.arange(BATCH), SEQ // GROUP), BATCH + jnp.arange(DEC_BATCH)])
    mod = _mod_call(c_all, w_mod, b_mod).reshape(DEPTH, N_CBATCH, N_MOD, D)
    modg_all = jnp.transpose(mod[:, group_row], (0, 2, 1, 3))

    pbias, sbias = _bias_call(rpb_table)

    outs = {n: [] for n in ("kp", "vp", "sp", "cp", "ks", "vs", "ss", "cs")}
    for l in range(DEPTH):
        lam_init = 0.8 - 0.6 * math.exp(-0.3 * l)
        modg = modg_all[l]
        x = _ffn_call(x, modg, norm_g4, w_up16, w_down16, l, 0, 0)
        p = _proj_call(x, modg, norm_g4, w_in16, l)

        oa_p, k_p = _attn_prompt_call(p, pbias, lam_qk, qk_norm2, subln_w, l, lam_init)
        oa_s, k_s = _attn_sample_call(p, sbias, ck, cv, lam_qk, qk_norm2, subln_w, l, lam_init)
        og_p, s_p, c_p = _gdn_call(p, hist_p, s0_p, conv_w, a_log_p, dt_bias_p, gdn_w, l, BATCH, SEQ // CHUNK, 0)
        og_s, s_s, c_s = _gdn_call(p, hist_s[l], state_gdn[l], conv_w, a_log_p, dt_bias_p, gdn_w, l,
                                   DEC_BATCH, 1, P_TOK // CHUNK)

        x = _out_call(x, modg, jnp.concatenate([oa_p, oa_s], axis=0), jnp.concatenate([og_p, og_s], axis=0), w_out16, l)
        x = _ffn_call(x, modg, norm_g4, w_up16, w_down16, l, 1, 2)

        v_all = p[:, COL_V:COL_V + D_ATT]
        outs["kp"].append(k_p.reshape(BATCH, SEQ, H, HD))
        outs["vp"].append(v_all[:P_TOK].reshape(BATCH, SEQ, H, HD))
        outs["sp"].append(s_p)
        outs["cp"].append(c_p[:, 8 - (CONV_W - 1):])
        outs["ks"].append(k_s.reshape(DEC_BATCH, DEC_SEQ, H, HD))
        outs["vs"].append(v_all[P_TOK:].reshape(DEC_BATCH, DEC_SEQ, H, HD))
        outs["ss"].append(s_s)
        outs["cs"].append(c_s[:, 8 - (CONV_W - 1):])

    st = {n: jnp.stack(v) for n, v in outs.items()}
    return (x[:P_TOK].reshape(BATCH, SEQ, D), x[P_TOK:].reshape(DEC_BATCH, DEC_SEQ, D),
            st["kp"], st["vp"], st["sp"], st["cp"], st["ks"], st["vs"], st["ss"], st["cs"])
```

```python
import functools
import math

import jax
import jax.numpy as jnp
from jax import lax
from jax.experimental import pallas as pl
from jax.experimental.pallas import tpu as pltpu

f32 = jnp.float32
bf16 = jnp.bfloat16

D = 2048
BATCH = 2
SEQ = 4096
DEPTH = 2
DEC_BATCH = 32
DEC_SEQ = 64
PAST = 2048
CHUNK = 64
H = 8
HD = 128
HALF = 64
D_ATT = H * HD
D_GDN = H * HD
CONV_W = 4
CONV_CH = 3 * D_GDN
D_FF = 5632
NUM_BUCKETS = 32
N_MOD = 9
N_IN = 3 * D_ATT + CONV_CH + D_GDN + 2 * H
EPS = 1e-6

GROUP = 64
P_TOK = BATCH * SEQ
S_TOK = DEC_BATCH * DEC_SEQ
TOKENS = P_TOK + S_TOK
N_GROUPS = TOKENS // GROUP
N_CBATCH = 40
N_PROJ = 7680
COL_K, COL_V, COL_CONV, COL_Z, COL_AB = D_ATT, 2 * D_ATT, 3 * D_ATT, 3 * D_ATT + CONV_CH, 3 * D_ATT + CONV_CH + D_GDN

MASKED = -1e30

FFN_TM, FFN_FC = 512, 512
PROJ_TM, PROJ_TN = 1024, 512
OUT_TM = 512
MOD_TN = 1024
ATT_TQ = 256
ATT_TK = 256
SATT_TK = 512
GDN_SLOTS = 2


def _params(sem, vmem_mb):
    return pltpu.CompilerParams(dimension_semantics=sem, vmem_limit_bytes=vmem_mb * 1024 * 1024)


def _dot(a, b):
    return jnp.dot(a, b, preferred_element_type=f32)


def _dot_nt(a, b):
    return lax.dot_general(a, b, (((1,), (1,)), ((), ())), preferred_element_type=f32)


def _silu(x):
    return x * jax.nn.sigmoid(x)


def _modulated_norm(x, ng, scale, shift):
    tm = x.shape[0]
    ms = jnp.mean(x * x, axis=-1, keepdims=True)
    y = x * lax.rsqrt(ms + EPS) * ng
    y = y.reshape(tm // GROUP, GROUP, D) * (1.0 + scale[:, None, :]) + shift[:, None, :]
    return y.reshape(tm, D)


def _gated(y, gate):
    tm = y.shape[0]
    return (y.reshape(tm // GROUP, GROUP, D) * gate[:, None, :]).reshape(tm, D)


def _blockdiag_half():
    r = lax.broadcasted_iota(jnp.int32, (HD, HD), 0) // HALF
    c = lax.broadcasted_iota(jnp.int32, (HD, HD), 1) // HALF
    return (r == c).astype(bf16)


def _half_rms(x, w, bd):
    ss = _dot((x * x).astype(bf16), bd) * (1.0 / HALF)
    return x * lax.rsqrt(ss + EPS) * w


def _mod_kernel(c_ref, w_ref, b_ref, o_ref):
    a = _silu(c_ref[...]).astype(bf16)
    o_ref[...] = _dot(a, w_ref[...].astype(bf16)) + b_ref[...]


def _mod_call(c_all, w_mod, b_mod):
    n = N_MOD * D // MOD_TN
    return pl.pallas_call(
        _mod_kernel,
        grid=(DEPTH, n),
        in_specs=[
            pl.BlockSpec((N_CBATCH, D), lambda l, j: (0, 0)),
            pl.BlockSpec((None, D, MOD_TN), lambda l, j: (l, 0, j)),
            pl.BlockSpec((None, 1, MOD_TN), lambda l, j: (l, 0, j)),
        ],
        out_specs=pl.BlockSpec((None, N_CBATCH, MOD_TN), lambda l, j: (l, 0, j)),
        out_shape=jax.ShapeDtypeStruct((DEPTH, N_CBATCH, N_MOD * D), f32),
        compiler_params=_params(("arbitrary", "arbitrary"), 40),
        name="adaln_mod",
    )(c_all, w_mod, b_mod.reshape(DEPTH, 1, N_MOD * D))


def _ffn_kernel(x_ref, shift_ref, scale_ref, gate_ref, ng_ref, wg_ref, wu_ref, wd_ref, o_ref, h_scr, acc_scr):
    k = pl.program_id(1)

    @pl.when(k == 0)
    def _():
        h_scr[...] = _modulated_norm(x_ref[...], ng_ref[...], scale_ref[...], shift_ref[...]).astype(bf16)
        acc_scr[...] = jnp.zeros_like(acc_scr)

    h = h_scr[...]
    g = _dot(h, wg_ref[...])
    u = _dot(h, wu_ref[...])
    acc_scr[...] += _dot((_silu(g) * u).astype(bf16), wd_ref[...])

    @pl.when(k == pl.num_programs(1) - 1)
    def _():
        o_ref[...] = x_ref[...] + 0.5 * _gated(acc_scr[...], gate_ref[...])


def _ffn_call(x, modg, norm_g, w_up, w_down, layer, idx, j):
    nk = D_FF // FFN_FC
    gm = FFN_TM // GROUP
    row = lambda r: pl.BlockSpec((None, gm, D), lambda m, k: (r, m, 0))
    return pl.pallas_call(
        _ffn_kernel,
        grid=(TOKENS // FFN_TM, nk),
        in_specs=[
            pl.BlockSpec((FFN_TM, D), lambda m, k: (m, 0)),
            row(3 * j), row(3 * j + 1), row(3 * j + 2),
            pl.BlockSpec((None, None, 1, D), lambda m, k: (layer, j, 0, 0)),
            pl.BlockSpec((None, None, D, FFN_FC), lambda m, k: (layer, idx, 0, k)),
            pl.BlockSpec((None, None, D, FFN_FC), lambda m, k: (layer, idx, 0, nk + k)),
            pl.BlockSpec((None, None, FFN_FC, D), lambda m, k: (layer, idx, k, 0)),
        ],
        out_specs=pl.BlockSpec((FFN_TM, D), lambda m, k: (m, 0)),
        out_shape=jax.ShapeDtypeStruct((TOKENS, D), f32),
        scratch_shapes=[pltpu.VMEM((FFN_TM, D), bf16), pltpu.VMEM((FFN_TM, D), f32)],
        compiler_params=_params(("arbitrary", "arbitrary"), 48),
        name="swiglu_half_step",
    )(x, modg, modg, modg, norm_g, w_up, w_up, w_down)


def _proj_kernel(x_ref, shift_ref, scale_ref, ng_ref, w_ref, o_ref, h_scr):
    @pl.when(pl.program_id(1) == 0)
    def _():
        h_scr[...] = _modulated_norm(x_ref[...], ng_ref[...], scale_ref[...], shift_ref[...]).astype(bf16)

    o_ref[...] = _dot(h_scr[...], w_ref[...])


def _proj_call(x, modg, norm_g, w_in, layer):
    gm = PROJ_TM // GROUP
    row = lambda r: pl.BlockSpec((None, gm, D), lambda m, n: (r, m, 0))
    return pl.pallas_call(
        _proj_kernel,
        grid=(TOKENS // PROJ_TM, N_PROJ // PROJ_TN),
        in_specs=[
            pl.BlockSpec((PROJ_TM, D), lambda m, n: (m, 0)),
            row(3), row(4),
            pl.BlockSpec((None, None, 1, D), lambda m, n: (layer, 1, 0, 0)),
            pl.BlockSpec((None, D, PROJ_TN), lambda m, n: (layer, 0, n)),
        ],
        out_specs=pl.BlockSpec((PROJ_TM, PROJ_TN), lambda m, n: (m, n)),
        out_shape=jax.ShapeDtypeStruct((TOKENS, N_PROJ), f32),
        scratch_shapes=[pltpu.VMEM((PROJ_TM, D), bf16)],
        compiler_params=_params(("arbitrary", "arbitrary"), 40),
        name="mixer_in_proj",
    )(x, modg, modg, norm_g, w_in)


def _out_kernel(x_ref, gate_ref, a_ref, g_ref, w_ref, o_ref):
    y = _dot(a_ref[...], w_ref[:D_ATT, :]) + _dot(g_ref[...], w_ref[D_ATT:, :])
    o_ref[...] = x_ref[...] + _gated(y, gate_ref[...])


def _out_call(x, modg, o_att, o_gdn, w_out, layer):
    gm = OUT_TM // GROUP
    return pl.pallas_call(
        _out_kernel,
        grid=(TOKENS // OUT_TM,),
        in_specs=[
            pl.BlockSpec((OUT_TM, D), lambda m: (m, 0)),
            pl.BlockSpec((None, gm, D), lambda m: (5, m, 0)),
            pl.BlockSpec((OUT_TM, D_ATT), lambda m: (m, 0)),
            pl.BlockSpec((OUT_TM, D_GDN), lambda m: (m, 0)),
            pl.BlockSpec((None, D, D), lambda m: (layer, 0, 0)),
        ],
        out_specs=pl.BlockSpec((OUT_TM, D), lambda m: (m, 0)),
        out_shape=jax.ShapeDtypeStruct((TOKENS, D), f32),
        compiler_params=_params(("arbitrary",), 48),
        name="mixer_out_proj",
    )(x, modg, o_att, o_gdn, w_out)


def _bias_from_rel(rel, table_ref, h):
    n = jnp.abs(rel)
    big = jnp.full_like(n, 8)
    for t in (12, 16, 23, 32, 46, 64, 91):
        big = big + (n >= t).astype(jnp.int32)
    bucket = jnp.where(n < 8, n, big) + jnp.where(rel > 0, NUM_BUCKETS // 2, 0)
    out = jnp.zeros(rel.shape, f32)
    for b in range(NUM_BUCKETS):
        out = jnp.where(bucket == b, table_ref[b, h], out)
    return out


def _bias_kernel(table_ref, pb_ref, sb_ref):
    h = pl.program_id(0)
    qi = lax.broadcasted_iota(jnp.int32, (ATT_TQ, ATT_TK), 0)
    kj = lax.broadcasted_iota(jnp.int32, (ATT_TQ, ATT_TK), 1)
    for d in range(3):
        tile = _bias_from_rel(kj - qi - d * ATT_TQ, table_ref, h)
        if d == 0:
            tile = jnp.where(kj // CHUNK <= qi // CHUNK, tile, MASKED)
        pb_ref[d, :ATT_TQ, :] = tile
        pb_ref[d, ATT_TQ:, :] = tile
    n_keys = PAST + HD
    sq = lax.broadcasted_iota(jnp.int32, (DEC_SEQ, n_keys), 0)
    sk = lax.broadcasted_iota(jnp.int32, (DEC_SEQ, n_keys), 1)
    tile = _bias_from_rel(sk - (PAST + sq), table_ref, h)
    tile = jnp.where(sk < PAST + DEC_SEQ, tile, MASKED)
    sb_ref[:DEC_SEQ, :] = tile
    sb_ref[DEC_SEQ:, :] = tile


def _bias_call(rpb_table):
    return pl.pallas_call(
        _bias_kernel,
        grid=(H,),
        in_specs=[pl.BlockSpec(memory_space=pltpu.SMEM)],
        out_specs=[
            pl.BlockSpec((None, 3, 2 * ATT_TQ, ATT_TK), lambda h: (h, 0, 0, 0)),
            pl.BlockSpec((None, 2 * DEC_SEQ, PAST + HD), lambda h: (h, 0, 0)),
        ],
        out_shape=[
            jax.ShapeDtypeStruct((H, 3, 2 * ATT_TQ, ATT_TK), f32),
            jax.ShapeDtypeStruct((H, 2 * DEC_SEQ, PAST + HD), f32),
        ],
        compiler_params=_params(("arbitrary",), 32),
        name="rel_pos_bias_tiles",
    )(rpb_table)


def _two_map_queries(qn):
    lane = lax.broadcasted_iota(jnp.int32, qn.shape, 1)
    return jnp.concatenate([jnp.where(lane < HALF, qn, 0.0), jnp.where(lane >= HALF, qn, 0.0)], axis=0).astype(bf16)


def _softmax_step(s, v, m, l, acc):
    m_new = jnp.maximum(m, jnp.max(s, axis=-1, keepdims=True))
    p = jnp.exp(s - m_new)
    alpha = jnp.exp(m - m_new)
    l = alpha * l + jnp.sum(p, axis=-1, keepdims=True)
    acc = alpha * acc + _dot(p.astype(bf16), v)
    return m_new, l, acc


def _diff_combine(l, acc, lam_ref, sw, lam_init):
    n = acc.shape[0] // 2
    lq = lam_ref[...]
    lam = (jnp.exp(jnp.sum(lq[0:1] * lq[1:2], axis=-1, keepdims=True))
           - jnp.exp(jnp.sum(lq[2:3] * lq[3:4], axis=-1, keepdims=True)) + lam_init)
    o = acc[:n] / l[:n] - lam * (acc[n:] / l[n:])
    o = o * lax.rsqrt(jnp.mean(o * o, axis=-1, keepdims=True) + EPS) * sw
    return o * (1.0 - lam_init)


def _attn_prompt_kernel(lam_ref, qw_ref, kw_ref, sw_ref, bias_ref, q_ref, k_ref, v_ref,
                        o_ref, knew_ref, kn_scr, vb_scr, *, lam_init):
    qi = pl.program_id(2)
    bd = _blockdiag_half()

    @pl.when(qi == 0)
    def _():
        kn = _half_rms(k_ref[...], kw_ref[...], bd)
        knew_ref[...] = kn
        kn_scr[...] = kn.astype(bf16)
        vb_scr[...] = v_ref[...].astype(bf16)

    q2 = _two_map_queries(_half_rms(q_ref[...], qw_ref[...], bd) * (HALF ** -0.5))

    def body(j, carry):
        rows = pl.ds(pl.multiple_of(j * ATT_TK, ATT_TK), ATT_TK)
        s = _dot_nt(q2, kn_scr[rows, :]) + bias_ref[jnp.minimum(qi - j, 2)]
        return _softmax_step(s, vb_scr[rows, :], *carry)

    init = (jnp.full((2 * ATT_TQ, 1), MASKED, f32), jnp.zeros((2 * ATT_TQ, 1), f32), jnp.zeros((2 * ATT_TQ, HD), f32))
    _, l, acc = lax.fori_loop(0, qi + 1, body, init)
    o_ref[...] = _diff_combine(l, acc, lam_ref, sw_ref[...], lam_init).astype(bf16)


def _attn_prompt_call(p, bias, lam_qk, qk_norm2, subln_w, layer, lam_init):
    nq = SEQ // ATT_TQ
    vec = lambda r: pl.BlockSpec((None, None, 1, HD), lambda b, h, i: (layer, r, 0, 0))
    return pl.pallas_call(
        functools.partial(_attn_prompt_kernel, lam_init=lam_init),
        grid=(BATCH, H, nq),
        in_specs=[
            pl.BlockSpec((None, 4, HALF), lambda b, h, i: (layer, 0, 0)),
            vec(0), vec(1),
            pl.BlockSpec((None, 1, HD), lambda b, h, i: (layer, 0, 0)),
            pl.BlockSpec((None, 3, 2 * ATT_TQ, ATT_TK), lambda b, h, i: (h, 0, 0, 0)),
            pl.BlockSpec((ATT_TQ, HD), lambda b, h, i: (b * nq + i, h)),
            pl.BlockSpec((SEQ, HD), lambda b, h, i: (b, H + h)),
            pl.BlockSpec((SEQ, HD), lambda b, h, i: (b, 2 * H + h)),
        ],
        out_specs=[
            pl.BlockSpec((ATT_TQ, HD), lambda b, h, i: (b * nq + i, h)),
            pl.BlockSpec((SEQ, HD), lambda b, h, i: (b, h)),
        ],
        out_shape=[
            jax.ShapeDtypeStruct((P_TOK, D_ATT), bf16),
            jax.ShapeDtypeStruct((P_TOK, D_ATT), f32),
        ],
        scratch_shapes=[pltpu.VMEM((SEQ, HD), bf16), pltpu.VMEM((SEQ, HD), bf16)],
        compiler_params=_params(("arbitrary", "arbitrary", "arbitrary"), 40),
        name="diff_attn_prompt",
    )(lam_qk, qk_norm2, qk_norm2, subln_w.reshape(DEPTH, 1, HD), bias, p, p, p)


def _attn_sample_kernel(lam_ref, qw_ref, kw_ref, sw_ref, bias_ref, bnew_ref, q_ref, kx_ref, vx_ref, ck_ref, cv_ref,
                        o_ref, knew_ref, q2_scr, kn_scr, m_scr, l_scr, acc_scr, *, lam_init):
    ks = pl.program_id(1)
    last = pl.num_programs(1) - 1
    bd = _blockdiag_half()

    @pl.when(ks == 0)
    def _():
        m_scr[...] = jnp.full_like(m_scr, MASKED)
        l_scr[...] = jnp.zeros_like(l_scr)
        acc_scr[...] = jnp.zeros_like(acc_scr)
        for h in range(H):
            cols = slice(h * HD, (h + 1) * HD)
            q2_scr[h] = _two_map_queries(_half_rms(q_ref[:, cols], qw_ref[...], bd) * (HALF ** -0.5))
            kn = _half_rms(kx_ref[:, cols], kw_ref[...], bd)
            knew_ref[:, cols] = kn
            kn_scr[:, cols] = kn.astype(bf16)

    for h in range(H):
        head_rows = pl.ds(h, SATT_TK, stride=H)
        s = _dot_nt(q2_scr[h], ck_ref[head_rows, :].astype(bf16)) + bias_ref[h]
        m_scr[h], l_scr[h], acc_scr[h] = _softmax_step(s, cv_ref[head_rows, :].astype(bf16),
                                                       m_scr[h], l_scr[h], acc_scr[h])

    @pl.when(ks == last)
    def _():
        vnew = vx_ref[...].astype(bf16)
        for h in range(H):
            cols = slice(h * HD, (h + 1) * HD)
            s = _dot_nt(q2_scr[h], kn_scr[:, cols]) + bnew_ref[h][:, :DEC_SEQ]
            _, l, acc = _softmax_step(s, vnew[:, cols], m_scr[h], l_scr[h], acc_scr[h])
            o_ref[:, cols] = _diff_combine(l, acc, lam_ref, sw_ref[...], lam_init).astype(bf16)


def _attn_sample_call(p, bias, cache_k, cache_v, lam_qk, qk_norm2, subln_w, layer, lam_init):
    nks = PAST // SATT_TK
    row0 = P_TOK // DEC_SEQ
    vec = lambda r: pl.BlockSpec((None, None, 1, HD), lambda b, s: (layer, r, 0, 0))
    sec = lambda c: pl.BlockSpec((DEC_SEQ, D_ATT), lambda b, s: (row0 + b, c))
    cache = pl.BlockSpec((None, None, SATT_TK * H, HD), lambda b, s: (layer, b, s, 0))
    return pl.pallas_call(
        functools.partial(_attn_sample_kernel, lam_init=lam_init),
        grid=(DEC_BATCH, nks),
        in_specs=[
            pl.BlockSpec((None, 4, HALF), lambda b, s: (layer, 0, 0)),
            vec(0), vec(1),
            pl.BlockSpec((None, 1, HD), lambda b, s: (layer, 0, 0)),
            pl.BlockSpec((H, 2 * DEC_SEQ, SATT_TK), lambda b, s: (0, 0, s)),
            pl.BlockSpec((H, 2 * DEC_SEQ, HD), lambda b, s: (0, 0, PAST // HD)),
            sec(0), sec(1), sec(2), cache, cache,
        ],
        out_specs=[
            pl.BlockSpec((DEC_SEQ, D_ATT), lambda b, s: (b, 0)),
            pl.BlockSpec((DEC_SEQ, D_ATT), lambda b, s: (b, 0)),
        ],
        out_shape=[
            jax.ShapeDtypeStruct((S_TOK, D_ATT), bf16),
            jax.ShapeDtypeStruct((S_TOK, D_ATT), f32),
        ],
        scratch_shapes=[
            pltpu.VMEM((H, 2 * DEC_SEQ, HD), bf16),
            pltpu.VMEM((DEC_SEQ, D_ATT), bf16),
            pltpu.VMEM((H, 2 * DEC_SEQ, 1), f32),
            pltpu.VMEM((H, 2 * DEC_SEQ, 1), f32),
            pltpu.VMEM((H, 2 * DEC_SEQ, HD), f32),
        ],
        compiler_params=_params(("arbitrary", "arbitrary"), 48),
        name="diff_attn_sample",
    )(lam_qk, qk_norm2, qk_norm2, subln_w.reshape(DEPTH, 1, HD), bias, bias, p, p, p, cache_k, cache_v)


def _gdn_kernel(p_ref, z_ref, ab_ref, hist_ref, s0_ref, cw_ref, alog_ref, dtb_ref, gw_ref,
                o_ref, sfin_ref, cnew_ref, xp_scr, s_scr, *, n_seq):
    per_seq = GDN_SLOTS // n_seq
    rows_seq = per_seq * CHUNK

    @pl.when(pl.program_id(1) == 0)
    def _():
        xp_scr[:, 0:8, :] = hist_ref[...]
        s_scr[...] = s0_ref[...]

    ys = []
    for sq in range(n_seq):
        xp_scr[sq, 8:8 + rows_seq, :] = p_ref[sq * rows_seq:(sq + 1) * rows_seq, :]
        y = xp_scr[sq, 5:5 + rows_seq, :] * cw_ref[0:1, :]
        for i in range(1, CONV_W):
            y = y + xp_scr[sq, 5 + i:5 + i + rows_seq, :] * cw_ref[i:i + 1, :]
        y = _silu(y)
        tail = xp_scr[sq, rows_seq:rows_seq + 8, :]
        xp_scr[sq, 0:8, :] = tail
        cnew_ref[sq] = tail
        ys += [y[c * CHUNK:(c + 1) * CHUNK] for c in range(per_seq)]

    ii = lax.broadcasted_iota(jnp.int32, (CHUNK, CHUNK), 0)
    jj = lax.broadcasted_iota(jnp.int32, (CHUNK, CHUNK), 1)
    causal = ii >= jj
    strict = ii > jj
    eye = jnp.where(ii == jj, 1.0, 0.0)

    units = [(n, h) for n in range(GDN_SLOTS) for h in range(H)]
    q16, k16, kb16, rhs16, gamma, q_dec, k_dec, decay = {}, {}, {}, {}, {}, {}, {}, {}
    for n in range(GDN_SLOTS):
        y = ys[n]
        ab = ab_ref[n * CHUNK:(n + 1) * CHUNK, :]
        pre = ab + dtb_ref[...]
        softplus = jnp.maximum(pre, 0.0) + jnp.log1p(jnp.exp(-jnp.abs(pre)))
        g_all = -jnp.exp(alog_ref[...]) * softplus
        beta_all = jax.nn.sigmoid(ab)
        for h in range(H):
            u = (n, h)
            qx = y[:, h * HD:(h + 1) * HD]
            kx = y[:, D_GDN + h * HD:D_GDN + (h + 1) * HD]
            v = y[:, 2 * D_GDN + h * HD:2 * D_GDN + (h + 1) * HD]
            q = qx * lax.rsqrt(jnp.sum(qx * qx, axis=-1, keepdims=True) + EPS) * (HD ** -0.5)
            k = kx * lax.rsqrt(jnp.sum(kx * kx, axis=-1, keepdims=True) + EPS)
            g = g_all[:, h:h + 1]
            beta = beta_all[:, H + h:H + h + 1]
            gc_row = jnp.sum(jnp.where(ii <= jj, g, 0.0), axis=0, keepdims=True)
            gc = jnp.sum(jnp.where(ii == jj, gc_row, 0.0), axis=1, keepdims=True)
            gc_last = jnp.sum(g, axis=0, keepdims=True)
            gamma[u] = jnp.where(causal, jnp.exp(jnp.where(causal, gc - gc_row, 0.0)), 0.0)
            egc = jnp.exp(gc)
            kb = k * beta
            q16[u], k16[u], kb16[u] = q.astype(bf16), k.astype(bf16), kb.astype(bf16)
            rhs16[u] = jnp.concatenate([kb * egc, v * beta], axis=-1).astype(bf16)
            q_dec[u] = q * egc
            k_dec[u] = k * jnp.exp(gc_last - gc)
            decay[u] = jnp.exp(gc_last)

    kk = {u: _dot_nt(kb16[u], k16[u]) for u in units}
    qk = {u: _dot_nt(q16[u], k16[u]) for u in units}
    a = {u: jnp.where(strict, kk[u] * gamma[u], 0.0) for u in units}
    attn = {u: qk[u] * gamma[u] for u in units}

    base = 8
    d = {u: jnp.where(ii // base == jj // base, a[u], 0.0) for u in units}
    t = {u: eye - d[u] for u in units}
    d16 = {u: d[u].astype(bf16) for u in units}
    pw16 = {u: _dot(d16[u], d16[u]).astype(bf16) for u in units}
    t = {u: t[u] + _dot(t[u].astype(bf16), pw16[u]) for u in units}
    pw16 = {u: _dot(pw16[u], pw16[u]).astype(bf16) for u in units}
    t = {u: t[u] + _dot(t[u].astype(bf16), pw16[u]) for u in units}
    size = base
    while size < CHUNK:
        off = (ii // (2 * size) == jj // (2 * size)) & (ii // size != jj // size)
        t16 = {u: t[u].astype(bf16) for u in units}
        ta = {u: _dot(t16[u], jnp.where(off, a[u], 0.0).astype(bf16)) for u in units}
        t = {u: t[u] - _dot(ta[u].astype(bf16), t16[u]) for u in units}
        size *= 2
    sol = {u: _dot(t[u].astype(bf16), rhs16[u]) for u in units}

    s_cur = {}
    for n in range(GDN_SLOTS):
        sq = n // per_seq
        if n % per_seq == 0:
            s_cur = {h: s_scr[sq, h] for h in range(H)}
        ws = {h: _dot(jnp.concatenate([sol[(n, h)][:, :HD], q_dec[(n, h)]], axis=0).astype(bf16), s_cur[h].astype(bf16))
              for h in range(H)}
        vn16 = {h: (sol[(n, h)][:, HD:] - ws[h][:CHUNK]).astype(bf16) for h in range(H)}
        r = {h: _dot(jnp.concatenate([attn[(n, h)], k_dec[(n, h)].T], axis=0).astype(bf16), vn16[h]) for h in range(H)}
        for h in range(H):
            cols = slice(h * HD, (h + 1) * HD)
            rows = slice(n * CHUNK, (n + 1) * CHUNK)
            s_cur[h] = s_cur[h] * decay[(n, h)] + r[h][CHUNK:]
            o = ws[h][CHUNK:] + r[h][:CHUNK]
            o = o * lax.rsqrt(jnp.mean(o * o, axis=-1, keepdims=True) + EPS) * gw_ref[...]
            o_ref[rows, cols] = (o * _silu(z_ref[rows, cols])).astype(bf16)
        if n % per_seq == per_seq - 1:
            for h in range(H):
                s_scr[sq, h] = s_cur[h]

    @pl.when(pl.program_id(1) == pl.num_programs(1) - 1)
    def _():
        sfin_ref[...] = s_scr[...]


def _gdn_call(p, hist, s0, conv_w, a_log, dt_bias, gdn_norm_w, layer, n_blocks, steps, n_seq, row0):
    br = GDN_SLOTS * CHUNK
    rows = lambda col, width: pl.BlockSpec((br, width), lambda b, c: (row0 + b * steps + c, col))
    vec = pl.BlockSpec((None, 1, HD), lambda b, c: (layer, 0, 0))
    return pl.pallas_call(
        functools.partial(_gdn_kernel, n_seq=n_seq),
        grid=(n_blocks, steps),
        in_specs=[
            rows(COL_CONV // CONV_CH, CONV_CH),
            rows(COL_Z // D_GDN, D_GDN),
            rows(COL_AB // HD, HD),
            pl.BlockSpec((n_seq, 8, CONV_CH), lambda b, c: (b, 0, 0)),
            pl.BlockSpec((n_seq, H, HD, HD), lambda b, c: (b, 0, 0, 0)),
            pl.BlockSpec((None, CONV_W, CONV_CH), lambda b, c: (layer, 0, 0)),
            vec, vec, vec,
        ],
        out_specs=[
            pl.BlockSpec((br, D_GDN), lambda b, c: (b * steps + c, 0)),
            pl.BlockSpec((n_seq, H, HD, HD), lambda b, c: (b, 0, 0, 0)),
            pl.BlockSpec((n_seq, 8, CONV_CH), lambda b, c: (b, 0, 0)),
        ],
        out_shape=[
            jax.ShapeDtypeStruct((n_blocks * steps * br, D_GDN), bf16),
            jax.ShapeDtypeStruct((n_blocks * n_seq, H, HD, HD), f32),
            jax.ShapeDtypeStruct((n_blocks * n_seq, 8, CONV_CH), f32),
        ],
        scratch_shapes=[pltpu.VMEM((n_seq, 8 + br // n_seq, CONV_CH), f32), pltpu.VMEM((n_seq, H, HD, HD), f32)],
        compiler_params=_params(("arbitrary", "arbitrary"), 48),
        name="gated_deltanet",
    )(p, p, p, hist, s0, conv_w, a_log, dt_bias, gdn_norm_w)


def _pad_lanes(v):
    return jnp.pad(v, ((0, 0), (0, HD - v.shape[-1]))).reshape(DEPTH, 1, HD)


def kernel(x_prompt, x_sample, cache_k, cache_v, state_gdn, state_conv, c_prompt, c_sample, w_mod, b_mod, norm_g, ffn_up, ffn_down, w_in, w_out, qk_norm, lam_qk, subln_w, rpb_table, conv_w, a_log, dt_bias, gdn_norm_w):
    x = jnp.concatenate([x_prompt.reshape(P_TOK, D), x_sample.reshape(S_TOK, D)], axis=0)
    c_all = jnp.concatenate([c_prompt, c_sample, jnp.zeros((N_CBATCH - BATCH - DEC_BATCH, D), f32)], axis=0)

    w_up16 = ffn_up.astype(bf16)
    w_down16 = ffn_down.astype(bf16)
    w_in16 = jnp.pad(w_in.astype(bf16), ((0, 0), (0, 0), (0, N_PROJ - N_IN)))
    w_out16 = w_out.astype(bf16)
    norm_g4 = norm_g.reshape(DEPTH, 3, 1, D)
    qk_norm2 = jnp.concatenate([qk_norm, qk_norm], axis=-1).reshape(DEPTH, 2, 1, HD)
    gdn_w = gdn_norm_w.reshape(DEPTH, 1, HD)
    a_log_p, dt_bias_p = _pad_lanes(a_log), _pad_lanes(dt_bias)
    ck = cache_k.reshape(DEPTH, DEC_BATCH, PAST * H, HD)
    cv = cache_v.reshape(DEPTH, DEC_BATCH, PAST * H, HD)
    hist_s = jnp.pad(state_conv, ((0, 0), (0, 0), (8 - (CONV_W - 1), 0), (0, 0)))
    hist_p = jnp.zeros((BATCH, 8, CONV_CH), f32)
    s0_p = jnp.zeros((BATCH, H, HD, HD), f32)

    group_row = jnp.concatenate([jnp.repeat(jnp.arange(BATCH), SEQ // GROUP), BATCH + jnp.arange(DEC_BATCH)])
    mod = _mod_call(c_all, w_mod, b_mod).reshape(DEPTH, N_CBATCH, N_MOD, D)
    modg_all = jnp.transpose(mod[:, group_row], (0, 2, 1, 3))

    pbias, sbias = _bias_call(rpb_table)

    gdn_rows = GDN_SLOTS * CHUNK
    outs = {n: [] for n in ("kp", "vp", "sp", "cp", "ks", "vs", "ss", "cs")}
    for l in range(DEPTH):
        lam_init = 0.8 - 0.6 * math.exp(-0.3 * l)
        modg = modg_all[l]
        x = _ffn_call(x, modg, norm_g4, w_up16, w_down16, l, 0, 0)
        p = _proj_call(x, modg, norm_g4, w_in16, l)

        oa_p, k_p = _attn_prompt_call(p, pbias, lam_qk, qk_norm2, subln_w, l, lam_init)
        oa_s, k_s = _attn_sample_call(p, sbias, ck, cv, lam_qk, qk_norm2, subln_w, l, lam_init)
        og_p, s_p, c_p = _gdn_call(p, hist_p, s0_p, conv_w, a_log_p, dt_bias_p, gdn_w, l,
                                   BATCH, SEQ // gdn_rows, 1, 0)
        og_s, s_s, c_s = _gdn_call(p, hist_s[l], state_gdn[l], conv_w, a_log_p, dt_bias_p, gdn_w, l,
                                   DEC_BATCH // GDN_SLOTS, 1, GDN_SLOTS, P_TOK // gdn_rows)

        x = _out_call(x, modg, jnp.concatenate([oa_p, oa_s], axis=0), jnp.concatenate([og_p, og_s], axis=0), w_out16, l)
        x = _ffn_call(x, modg, norm_g4, w_up16, w_down16, l, 1, 2)

        v_all = p[:, COL_V:COL_V + D_ATT]
        outs["kp"].append(k_p.reshape(BATCH, SEQ, H, HD))
        outs["vp"].append(v_all[:P_TOK].reshape(BATCH, SEQ, H, HD))
        outs["sp"].append(s_p)
        outs["cp"].append(c_p[:, 8 - (CONV_W - 1):])
        outs["ks"].append(k_s.reshape(DEC_BATCH, DEC_SEQ, H, HD))
        outs["vs"].append(v_all[P_TOK:].reshape(DEC_BATCH, DEC_SEQ, H, HD))
        outs["ss"].append(s_s)
        outs["cs"].append(c_s[:, 8 - (CONV_W - 1):])

    st = {n: jnp.stack(v) for n, v in outs.items()}
    return (x[:P_TOK].reshape(BATCH, SEQ, D), x[P_TOK:].reshape(DEC_BATCH, DEC_SEQ, D),
            st["kp"], st["vp"], st["sp"], st["cp"], st["ks"], st["vs"], st["ss"], st["cs"])
```

```python
import functools
import math

import jax
import jax.numpy as jnp
from jax import lax
from jax.experimental import pallas as pl
from jax.experimental.pallas import tpu as pltpu

f32 = jnp.float32
bf16 = jnp.bfloat16

D = 2048
BATCH = 2
SEQ = 4096
DEPTH = 2
DEC_BATCH = 32
DEC_SEQ = 64
PAST = 2048
CHUNK = 64
H = 8
HD = 128
HALF = 64
D_ATT = H * HD
D_GDN = H * HD
CONV_W = 4
CONV_CH = 3 * D_GDN
D_FF = 5632
NUM_BUCKETS = 32
FAR_BUCKET = 15
FAR_DIST = 91
N_MOD = 9
N_IN = 3 * D_ATT + CONV_CH + D_GDN + 2 * H
EPS = 1e-6

GROUP = 64
P_TOK = BATCH * SEQ
S_TOK = DEC_BATCH * DEC_SEQ
N_CBATCH = 40
N_PROJ = 7680
COL_K, COL_V, COL_CONV, COL_Z, COL_AB = D_ATT, 2 * D_ATT, 3 * D_ATT, 3 * D_ATT + CONV_CH, 3 * D_ATT + CONV_CH + D_GDN

MASKED = -1e30
SUB = 128

FFN_TM, FFN_FC = 512, 512
PROJ_TM, PROJ_TN = 1024, 512
OUT_TM = 512
MOD_TN = 1024
ATT_TQ = 512
ATT_TK = ATT_TQ
ATT_NG = 4
ATT_LG = 2 * ATT_TQ // ATT_NG
LOG2E = math.log2(math.e)
SATT_TK = 512
GDN_SLOTS = 2


def _params(sem, vmem_mb):
    return pltpu.CompilerParams(dimension_semantics=sem, vmem_limit_bytes=vmem_mb * 1024 * 1024)


def _dot(a, b):
    return jnp.dot(a, b, preferred_element_type=f32)


def _dot_nt(a, b):
    return lax.dot_general(a, b, (((1,), (1,)), ((), ())), preferred_element_type=f32)


def _silu(x):
    return x * jax.nn.sigmoid(x)


def _modulated_norm(x, ng, scale, shift):
    tm = x.shape[0]
    ms = jnp.mean(x * x, axis=-1, keepdims=True)
    y = x * lax.rsqrt(ms + EPS) * ng
    y = y.reshape(tm // GROUP, GROUP, D) * (1.0 + scale[:, None, :]) + shift[:, None, :]
    return y.reshape(tm, D)


def _gated(y, gate):
    tm = y.shape[0]
    return (y.reshape(tm // GROUP, GROUP, D) * gate[:, None, :]).reshape(tm, D)


def _blockdiag_half():
    r = lax.broadcasted_iota(jnp.int32, (HD, HD), 0) // HALF
    c = lax.broadcasted_iota(jnp.int32, (HD, HD), 1) // HALF
    return (r == c).astype(bf16)


def _half_rms(x, w, bd):
    ss = _dot((x * x).astype(bf16), bd) * (1.0 / HALF)
    return x * lax.rsqrt(ss + EPS) * w


def _mod_kernel(c_ref, w_ref, b_ref, o_ref):
    a = _silu(c_ref[...]).astype(bf16)
    o_ref[...] = _dot(a, w_ref[...].astype(bf16)) + b_ref[...]


def _mod_call(c_all, w_mod, b_mod):
    n = N_MOD * D // MOD_TN
    return pl.pallas_call(
        _mod_kernel,
        grid=(DEPTH, n),
        in_specs=[
            pl.BlockSpec((N_CBATCH, D), lambda l, j: (0, 0)),
            pl.BlockSpec((None, D, MOD_TN), lambda l, j: (l, 0, j)),
            pl.BlockSpec((None, 1, MOD_TN), lambda l, j: (l, 0, j)),
        ],
        out_specs=pl.BlockSpec((None, N_CBATCH, MOD_TN), lambda l, j: (l, 0, j)),
        out_shape=jax.ShapeDtypeStruct((DEPTH, N_CBATCH, N_MOD * D), f32),
        compiler_params=_params(("arbitrary", "arbitrary"), 40),
        name="adaln_mod",
    )(c_all, w_mod, b_mod.reshape(DEPTH, 1, N_MOD * D))


def _ffn_kernel(x_ref, shift_ref, scale_ref, gate_ref, ng_ref, wg_ref, wu_ref, wd_ref, o_ref, h_scr, acc_scr):
    k = pl.program_id(1)

    @pl.when(k == 0)
    def _():
        h_scr[...] = _modulated_norm(x_ref[...], ng_ref[...], scale_ref[...], shift_ref[...]).astype(bf16)
        acc_scr[...] = jnp.zeros_like(acc_scr)

    h = h_scr[...]
    g = _dot(h, wg_ref[...])
    u = _dot(h, wu_ref[...])
    acc_scr[...] += _dot((_silu(g) * u).astype(bf16), wd_ref[...])

    @pl.when(k == pl.num_programs(1) - 1)
    def _():
        o_ref[...] = x_ref[...] + 0.5 * _gated(acc_scr[...], gate_ref[...])


def _ffn_call(x, modg, norm_g, w_up, w_down, layer, idx, j):
    n_tok = x.shape[0]
    nk = D_FF // FFN_FC
    gm = FFN_TM // GROUP
    row = lambda r: pl.BlockSpec((None, gm, D), lambda m, k: (r, m, 0))
    return pl.pallas_call(
        _ffn_kernel,
        grid=(n_tok // FFN_TM, nk),
        in_specs=[
            pl.BlockSpec((FFN_TM, D), lambda m, k: (m, 0)),
            row(3 * j), row(3 * j + 1), row(3 * j + 2),
            pl.BlockSpec((None, None, 1, D), lambda m, k: (layer, j, 0, 0)),
            pl.BlockSpec((None, None, D, FFN_FC), lambda m, k: (layer, idx, 0, k)),
            pl.BlockSpec((None, None, D, FFN_FC), lambda m, k: (layer, idx, 0, nk + k)),
            pl.BlockSpec((None, None, FFN_FC, D), lambda m, k: (layer, idx, k, 0)),
        ],
        out_specs=pl.BlockSpec((FFN_TM, D), lambda m, k: (m, 0)),
        out_shape=jax.ShapeDtypeStruct((n_tok, D), f32),
        scratch_shapes=[pltpu.VMEM((FFN_TM, D), bf16), pltpu.VMEM((FFN_TM, D), f32)],
        compiler_params=_params(("arbitrary", "arbitrary"), 48),
        name="swiglu_half_step",
    )(x, modg, modg, modg, norm_g, w_up, w_up, w_down)


def _proj_kernel(x_ref, shift_ref, scale_ref, ng_ref, w_ref, o_ref, h_scr):
    @pl.when(pl.program_id(1) == 0)
    def _():
        h_scr[...] = _modulated_norm(x_ref[...], ng_ref[...], scale_ref[...], shift_ref[...]).astype(bf16)

    o_ref[...] = _dot(h_scr[...], w_ref[...])


def _proj_call(x, modg, norm_g, w_in, layer):
    n_tok = x.shape[0]
    gm = PROJ_TM // GROUP
    row = lambda r: pl.BlockSpec((None, gm, D), lambda m, n: (r, m, 0))
    return pl.pallas_call(
        _proj_kernel,
        grid=(n_tok // PROJ_TM, N_PROJ // PROJ_TN),
        in_specs=[
            pl.BlockSpec((PROJ_TM, D), lambda m, n: (m, 0)),
            row(3), row(4),
            pl.BlockSpec((None, None, 1, D), lambda m, n: (layer, 1, 0, 0)),
            pl.BlockSpec((None, D, PROJ_TN), lambda m, n: (layer, 0, n)),
        ],
        out_specs=pl.BlockSpec((PROJ_TM, PROJ_TN), lambda m, n: (m, n)),
        out_shape=jax.ShapeDtypeStruct((n_tok, N_PROJ), f32),
        scratch_shapes=[pltpu.VMEM((PROJ_TM, D), bf16)],
        compiler_params=_params(("arbitrary", "arbitrary"), 40),
        name="mixer_in_proj",
    )(x, modg, modg, norm_g, w_in)


def _out_kernel(x_ref, gate_ref, a_ref, g_ref, w_ref, o_ref):
    y = _dot(a_ref[...], w_ref[:D_ATT, :]) + _dot(g_ref[...], w_ref[D_ATT:, :])
    o_ref[...] = x_ref[...] + _gated(y, gate_ref[...])


def _out_call(x, modg, o_att, o_gdn, w_out, layer):
    n_tok = x.shape[0]
    gm = OUT_TM // GROUP
    return pl.pallas_call(
        _out_kernel,
        grid=(n_tok // OUT_TM,),
        in_specs=[
            pl.BlockSpec((OUT_TM, D), lambda m: (m, 0)),
            pl.BlockSpec((None, gm, D), lambda m: (5, m, 0)),
            pl.BlockSpec((OUT_TM, D_ATT), lambda m: (m, 0)),
            pl.BlockSpec((OUT_TM, D_GDN), lambda m: (m, 0)),
            pl.BlockSpec((None, D, D), lambda m: (layer, 0, 0)),
        ],
        out_specs=pl.BlockSpec((OUT_TM, D), lambda m: (m, 0)),
        out_shape=jax.ShapeDtypeStruct((n_tok, D), f32),
        compiler_params=_params(("arbitrary",), 48),
        name="mixer_out_proj",
    )(x, modg, o_att, o_gdn, w_out)


def _bias_from_rel(rel, table_ref, h):
    n = jnp.abs(rel)
    big = jnp.full_like(n, 8)
    for t in (12, 16, 23, 32, 46, 64, FAR_DIST):
        big = big + (n >= t).astype(jnp.int32)
    bucket = jnp.where(n < 8, n, big) + jnp.where(rel > 0, NUM_BUCKETS // 2, 0)
    out = jnp.zeros(rel.shape, f32)
    for b in range(NUM_BUCKETS):
        out = jnp.where(bucket == b, table_ref[b, h], out)
    return out


def _fill_bias(dst_ref, lead, n_rows, n_cols, col0, key_axis, key_base, table_ref, h, scale=1.0):
    far = table_ref[FAR_BUCKET, h]
    for r0 in range(0, n_rows, SUB):
        nr = min(SUB, n_rows - r0)
        for c0 in range(0, n_cols, SUB):
            r = lax.broadcasted_iota(jnp.int32, (nr, SUB), 0) + r0
            c = lax.broadcasted_iota(jnp.int32, (nr, SUB), 1) + c0
            key, query = (r + key_base, c) if key_axis == 0 else (c + key_base, r)
            k_lo, q_lo = (r0 + key_base, c0) if key_axis == 0 else (c0 + key_base, r0)
            k_hi = k_lo + (nr if key_axis == 0 else SUB) - 1
            q_hi = q_lo + (SUB if key_axis == 0 else nr) - 1
            if k_lo // CHUNK > q_hi // CHUNK:
                tile = jnp.full((nr, SUB), MASKED, f32)
            elif k_hi - q_lo <= -FAR_DIST:
                tile = jnp.zeros((nr, SUB), f32)
            else:
                tile = (_bias_from_rel(key - query, table_ref, h) - far) * scale
                tile = jnp.where(key // CHUNK <= query // CHUNK, tile, MASKED)
            dst_ref[lead + (slice(r0, r0 + nr), slice(col0 + c0, col0 + c0 + SUB))] = tile


def _bias_kernel(table_ref, pb_ref, sb_ref):
    h = pl.program_id(0)
    _fill_bias(pb_ref, (0,), 2 * ATT_TQ, ATT_TQ, 0, 0, 0, table_ref, h, LOG2E)
    _fill_bias(pb_ref, (1,), 2 * ATT_TQ, ATT_TQ, 0, 0, -ATT_TQ, table_ref, h, LOG2E)
    _fill_bias(sb_ref, (), DEC_SEQ, SATT_TK, 0, 1, -SATT_TK, table_ref, h)
    _fill_bias(sb_ref, (), DEC_SEQ, HD, SATT_TK, 1, 0, table_ref, h)


def _bias_call(rpb_table):
    return pl.pallas_call(
        _bias_kernel,
        grid=(H,),
        in_specs=[pl.BlockSpec(memory_space=pltpu.SMEM)],
        out_specs=[
            pl.BlockSpec((None, 2, 2 * ATT_TQ, ATT_TQ), lambda h: (h, 0, 0, 0)),
            pl.BlockSpec((None, DEC_SEQ, SATT_TK + HD), lambda h: (h, 0, 0)),
        ],
        out_shape=[
            jax.ShapeDtypeStruct((H, 2, 2 * ATT_TQ, ATT_TQ), f32),
            jax.ShapeDtypeStruct((H, DEC_SEQ, SATT_TK + HD), f32),
        ],
        compiler_params=_params(("arbitrary",), 32),
        name="rel_pos_bias_tiles",
    )(rpb_table)


def _two_map_queries(qn):
    lane = lax.broadcasted_iota(jnp.int32, qn.shape, 1)
    return jnp.concatenate([jnp.where(lane < HALF, qn, 0.0), jnp.where(lane >= HALF, qn, 0.0)], axis=0).astype(bf16)


def _add_map_bias(s, b):
    n = b.shape[0]
    return (s.reshape(2, n, s.shape[1]) + b[None]).reshape(s.shape)


def _softmax_stats(s, m, l):
    m_new = jnp.maximum(m, jnp.max(s, axis=-1, keepdims=True))
    p = jnp.exp(s - m_new)
    alpha = jnp.exp(m - m_new)
    return m_new, alpha * l + jnp.sum(p, axis=-1, keepdims=True), alpha, p.astype(bf16)


def _diff_combine(l, acc, lam_ref, sw, lam_init):
    n = acc.shape[0] // 2
    lq = lam_ref[...]
    lam = (jnp.exp(jnp.sum(lq[0:1] * lq[1:2], axis=-1, keepdims=True))
           - jnp.exp(jnp.sum(lq[2:3] * lq[3:4], axis=-1, keepdims=True)) + lam_init)
    o = acc[:n] / l[:n] - lam * (acc[n:] / l[n:])
    o = o * lax.rsqrt(jnp.mean(o * o, axis=-1, keepdims=True) + EPS) * sw
    return o * (1.0 - lam_init)


def _attn_prompt_kernel(lam_ref, qw_ref, kw_ref, sw_ref, bias_ref, q_ref, k_ref, v_ref,
                        o_ref, knew_ref, vnew_ref, kn_scr, vt_scr, *, lam_init):
    qi = pl.program_id(2)
    bd = _blockdiag_half()

    @pl.when(qi == 0)
    def _():
        kn = _half_rms(k_ref[...], kw_ref[...], bd)
        knew_ref[...] = kn
        kn_scr[...] = kn.astype(bf16)
        for c in range(SEQ // ATT_TK):
            v = v_ref[c * ATT_TK:(c + 1) * ATT_TK, :]
            vnew_ref[c * ATT_TK:(c + 1) * ATT_TK, :] = v
            vt_scr[c] = v.T.astype(bf16)

    qt = (_half_rms(q_ref[...], qw_ref[...], bd) * (HALF ** -0.5 * LOG2E)).T
    feat = lax.broadcasted_iota(jnp.int32, qt.shape, 0)
    q2t = jnp.concatenate([jnp.where(feat < HALF, qt, 0.0), jnp.where(feat >= HALF, qt, 0.0)], axis=1).astype(bf16)
    q_groups = [q2t[:, g * ATT_LG:(g + 1) * ATT_LG] for g in range(ATT_NG)]

    def logits(j, groups):
        keys = kn_scr[pl.ds(pl.multiple_of(j * ATT_TK, ATT_TK), ATT_TK), :]
        return [_dot(keys, q_groups[g]) for g in groups]

    def consume(j, g, s, state, bias):
        m, l, acc = state
        if bias is not None:
            q0 = (g * ATT_LG) % ATT_TQ
            s = s + bias[:, q0:q0 + ATT_LG]
        m_new = jnp.maximum(m, jnp.max(s, axis=0, keepdims=True))
        p = jnp.exp2(s - m_new)
        alpha = jnp.exp2(m - m_new)
        l = alpha * l + jnp.sum(p, axis=0, keepdims=True)
        return m_new, l, alpha * acc + _dot(vt_scr[j], p.astype(bf16))

    def step(j, states, bias=None):
        s = logits(j, range(ATT_NG))
        return [consume(j, g, s[g], states[g], bias) for g in range(ATT_NG)]

    states = [(jnp.full((1, ATT_LG), MASKED, f32), jnp.zeros((1, ATT_LG), f32), jnp.zeros((HD, ATT_LG), f32))
              for _ in range(ATT_NG)]
    near0 = jnp.maximum(qi - 1, 0)
    states = lax.fori_loop(0, near0, step, states)
    variant = jnp.minimum(qi, 1)
    states = step(near0, states, bias_ref[variant, :ATT_TK, :])
    states = step(near0 + 1, states, bias_ref[variant, ATT_TK:, :])
    l = jnp.concatenate([st[1] for st in states], axis=1)
    acc = jnp.concatenate([st[2] for st in states], axis=1)

    lq = lam_ref[...]
    lam = (jnp.exp(jnp.sum(lq[0:1] * lq[1:2], axis=-1, keepdims=True))
           - jnp.exp(jnp.sum(lq[2:3] * lq[3:4], axis=-1, keepdims=True)) + lam_init)
    o2 = acc / l
    o = (o2[:, :ATT_TQ] - lam * o2[:, ATT_TQ:]).T
    o = o * lax.rsqrt(jnp.mean(o * o, axis=-1, keepdims=True) + EPS) * sw_ref[...]
    o_ref[...] = (o * (1.0 - lam_init)).astype(bf16)


def _attn_prompt_call(p, bias, lam_qk, qk_norm2, subln_w, layer, lam_init):
    nq = SEQ // ATT_TQ
    vec = lambda r: pl.BlockSpec((None, None, 1, HD), lambda b, h, i: (layer, r, 0, 0))
    head_seq = pl.BlockSpec((SEQ, HD), lambda b, h, i: (b, h))
    return pl.pallas_call(
        functools.partial(_attn_prompt_kernel, lam_init=lam_init),
        grid=(BATCH, H, nq),
        in_specs=[
            pl.BlockSpec((None, 4, HALF), lambda b, h, i: (layer, 0, 0)),
            vec(0), vec(1),
            pl.BlockSpec((None, 1, HD), lambda b, h, i: (layer, 0, 0)),
            pl.BlockSpec((None, 2, 2 * ATT_TQ, ATT_TQ), lambda b, h, i: (h, 0, 0, 0)),
            pl.BlockSpec((ATT_TQ, HD), lambda b, h, i: (b * nq + i, h)),
            pl.BlockSpec((SEQ, HD), lambda b, h, i: (b, H + h)),
            pl.BlockSpec((SEQ, HD), lambda b, h, i: (b, 2 * H + h)),
        ],
        out_specs=[pl.BlockSpec((ATT_TQ, HD), lambda b, h, i: (b * nq + i, h)), head_seq, head_seq],
        out_shape=[
            jax.ShapeDtypeStruct((P_TOK, D_ATT), bf16),
            jax.ShapeDtypeStruct((P_TOK, D_ATT), f32),
            jax.ShapeDtypeStruct((P_TOK, D_ATT), f32),
        ],
        scratch_shapes=[pltpu.VMEM((SEQ, HD), bf16), pltpu.VMEM((SEQ // ATT_TK, HD, ATT_TK), bf16)],
        compiler_params=_params(("arbitrary", "arbitrary", "arbitrary"), 56),
        name="diff_attn_prompt",
    )(lam_qk, qk_norm2, qk_norm2, subln_w.reshape(DEPTH, 1, HD), bias, p, p, p)


def _attn_sample_kernel(lam_ref, qw_ref, kw_ref, sw_ref, bias_ref, q_ref, kx_ref, vx_ref, ck_ref, cv_ref,
                        o_ref, knew_ref, vnew_ref, q2_scr, kn_scr, m_scr, l_scr, acc_scr, *, lam_init):
    ks = pl.program_id(1)
    last = pl.num_programs(1) - 1
    bd = _blockdiag_half()

    @pl.when(ks == 0)
    def _():
        m_scr[...] = jnp.full_like(m_scr, MASKED)
        l_scr[...] = jnp.zeros_like(l_scr)
        acc_scr[...] = jnp.zeros_like(acc_scr)
        vnew_ref[...] = vx_ref[...]
        for h in range(H):
            cols = slice(h * HD, (h + 1) * HD)
            q2_scr[h] = _two_map_queries(_half_rms(q_ref[:, cols], qw_ref[...], bd) * (HALF ** -0.5))
            kn = _half_rms(kx_ref[:, cols], kw_ref[...], bd)
            knew_ref[:, cols] = kn
            kn_scr[:, cols] = kn.astype(bf16)

    def head_rows(h):
        return pl.ds(h, SATT_TK, stride=H)

    def cached_block(bias_cols):
        ss = [_dot_nt(q2_scr[h], ck_ref[head_rows(h), :].astype(bf16)) for h in range(H)]
        if bias_cols is not None:
            ss = [_add_map_bias(ss[h], bias_ref[h, :, bias_cols]) for h in range(H)]
        stats = [_softmax_stats(ss[h], m_scr[h], l_scr[h]) for h in range(H)]
        pv = [_dot(stats[h][3], cv_ref[head_rows(h), :].astype(bf16)) for h in range(H)]
        for h in range(H):
            m_scr[h], l_scr[h] = stats[h][0], stats[h][1]
            acc_scr[h] = stats[h][2] * acc_scr[h] + pv[h]

    @pl.when(ks < last)
    def _():
        cached_block(None)

    @pl.when(ks == last)
    def _():
        cached_block(slice(0, SATT_TK))
        vnew = vx_ref[...].astype(bf16)
        ss = [_add_map_bias(_dot_nt(q2_scr[h], kn_scr[:, h * HD:(h + 1) * HD]), bias_ref[h, :, SATT_TK:SATT_TK + DEC_SEQ])
              for h in range(H)]
        stats = [_softmax_stats(ss[h], m_scr[h], l_scr[h]) for h in range(H)]
        pv = [_dot(stats[h][3], vnew[:, h * HD:(h + 1) * HD]) for h in range(H)]
        for h in range(H):
            acc = stats[h][2] * acc_scr[h] + pv[h]
            o_ref[:, h * HD:(h + 1) * HD] = _diff_combine(stats[h][1], acc, lam_ref, sw_ref[...], lam_init).astype(bf16)


def _attn_sample_call(p, bias, cache_k, cache_v, lam_qk, qk_norm2, subln_w, layer, lam_init):
    nks = PAST // SATT_TK
    vec = lambda r: pl.BlockSpec((None, None, 1, HD), lambda b, s: (layer, r, 0, 0))
    sec = lambda c: pl.BlockSpec((DEC_SEQ, D_ATT), lambda b, s: (b, c))
    cache = pl.BlockSpec((None, None, SATT_TK * H, HD), lambda b, s: (layer, b, s, 0))
    stream = pl.BlockSpec((DEC_SEQ, D_ATT), lambda b, s: (b, 0))
    return pl.pallas_call(
        functools.partial(_attn_sample_kernel, lam_init=lam_init),
        grid=(DEC_BATCH, nks),
        in_specs=[
            pl.BlockSpec((None, 4, HALF), lambda b, s: (layer, 0, 0)),
            vec(0), vec(1),
            pl.BlockSpec((None, 1, HD), lambda b, s: (layer, 0, 0)),
            pl.BlockSpec((H, DEC_SEQ, SATT_TK + HD), lambda b, s: (0, 0, 0)),
            sec(0), sec(1), sec(2), cache, cache,
        ],
        out_specs=[stream, stream, stream],
        out_shape=[
            jax.ShapeDtypeStruct((S_TOK, D_ATT), bf16),
            jax.ShapeDtypeStruct((S_TOK, D_ATT), f32),
            jax.ShapeDtypeStruct((S_TOK, D_ATT), f32),
        ],
        scratch_shapes=[
            pltpu.VMEM((H, 2 * DEC_SEQ, HD), bf16),
            pltpu.VMEM((DEC_SEQ, D_ATT), bf16),
            pltpu.VMEM((H, 2 * DEC_SEQ, 1), f32),
            pltpu.VMEM((H, 2 * DEC_SEQ, 1), f32),
            pltpu.VMEM((H, 2 * DEC_SEQ, HD), f32),
        ],
        compiler_params=_params(("arbitrary", "arbitrary"), 48),
        name="diff_attn_sample",
    )(lam_qk, qk_norm2, qk_norm2, subln_w.reshape(DEPTH, 1, HD), bias, p, p, p, cache_k, cache_v)


def _gdn_kernel(p_ref, z_ref, ab_ref, hist_ref, s0_ref, cw_ref, alog_ref, dtb_ref, gw_ref,
                o_ref, sfin_ref, cnew_ref, xp_scr, s_scr, *, n_seq):
    per_seq = GDN_SLOTS // n_seq
    rows_seq = per_seq * CHUNK

    @pl.when(pl.program_id(1) == 0)
    def _():
        xp_scr[:, 0:8, :] = hist_ref[...]
        s_scr[...] = s0_ref[...]

    ys = []
    for sq in range(n_seq):
        xp_scr[sq, 8:8 + rows_seq, :] = p_ref[sq * rows_seq:(sq + 1) * rows_seq, :]
        y = xp_scr[sq, 5:5 + rows_seq, :] * cw_ref[0:1, :]
        for i in range(1, CONV_W):
            y = y + xp_scr[sq, 5 + i:5 + i + rows_seq, :] * cw_ref[i:i + 1, :]
        y = _silu(y)
        tail = xp_scr[sq, rows_seq:rows_seq + 8, :]
        xp_scr[sq, 0:8, :] = tail
        cnew_ref[sq] = tail
        ys += [y[c * CHUNK:(c + 1) * CHUNK] for c in range(per_seq)]

    ii = lax.broadcasted_iota(jnp.int32, (CHUNK, CHUNK), 0)
    jj = lax.broadcasted_iota(jnp.int32, (CHUNK, CHUNK), 1)
    causal = ii >= jj
    strict = ii > jj
    eye = jnp.where(ii == jj, 1.0, 0.0)

    units = [(n, h) for n in range(GDN_SLOTS) for h in range(H)]
    q16, k16, kb16, rhs16, gamma, q_dec, k_dec, decay = {}, {}, {}, {}, {}, {}, {}, {}
    for n in range(GDN_SLOTS):
        y = ys[n]
        ab = ab_ref[n * CHUNK:(n + 1) * CHUNK, :]
        pre = ab + dtb_ref[...]
        softplus = jnp.maximum(pre, 0.0) + jnp.log1p(jnp.exp(-jnp.abs(pre)))
        g_all = -jnp.exp(alog_ref[...]) * softplus
        beta_all = jax.nn.sigmoid(ab)
        for h in range(H):
            u = (n, h)
            qx = y[:, h * HD:(h + 1) * HD]
            kx = y[:, D_GDN + h * HD:D_GDN + (h + 1) * HD]
            v = y[:, 2 * D_GDN + h * HD:2 * D_GDN + (h + 1) * HD]
            q = qx * lax.rsqrt(jnp.sum(qx * qx, axis=-1, keepdims=True) + EPS) * (HD ** -0.5)
            k = kx * lax.rsqrt(jnp.sum(kx * kx, axis=-1, keepdims=True) + EPS)
            g = g_all[:, h:h + 1]
            beta = beta_all[:, H + h:H + h + 1]
            gc_row = jnp.sum(jnp.where(ii <= jj, g, 0.0), axis=0, keepdims=True)
            gc = jnp.sum(jnp.where(ii == jj, gc_row, 0.0), axis=1, keepdims=True)
            gc_last = jnp.sum(g, axis=0, keepdims=True)
            gamma[u] = jnp.where(causal, jnp.exp(jnp.where(causal, gc - gc_row, 0.0)), 0.0)
            egc = jnp.exp(gc)
            kb = k * beta
            q16[u], k16[u], kb16[u] = q.astype(bf16), k.astype(bf16), kb.astype(bf16)
            rhs16[u] = jnp.concatenate([kb * egc, v * beta], axis=-1).astype(bf16)
            q_dec[u] = q * egc
            k_dec[u] = k * jnp.exp(gc_last - gc)
            decay[u] = jnp.exp(gc_last)

    kk = {u: _dot_nt(kb16[u], k16[u]) for u in units}
    qk = {u: _dot_nt(q16[u], k16[u]) for u in units}
    a = {u: jnp.where(strict, kk[u] * gamma[u], 0.0) for u in units}
    attn = {u: qk[u] * gamma[u] for u in units}

    base = 8
    d = {u: jnp.where(ii // base == jj // base, a[u], 0.0) for u in units}
    t = {u: eye - d[u] for u in units}
    d16 = {u: d[u].astype(bf16) for u in units}
    pw16 = {u: _dot(d16[u], d16[u]).astype(bf16) for u in units}
    t = {u: t[u] + _dot(t[u].astype(bf16), pw16[u]) for u in units}
    pw16 = {u: _dot(pw16[u], pw16[u]).astype(bf16) for u in units}
    t = {u: t[u] + _dot(t[u].astype(bf16), pw16[u]) for u in units}
    size = base
    while size < CHUNK:
        off = (ii // (2 * size) == jj // (2 * size)) & (ii // size != jj // size)
        t16 = {u: t[u].astype(bf16) for u in units}
        ta = {u: _dot(t16[u], jnp.where(off, a[u], 0.0).astype(bf16)) for u in units}
        t = {u: t[u] - _dot(ta[u].astype(bf16), t16[u]) for u in units}
        size *= 2
    sol = {u: _dot(t[u].astype(bf16), rhs16[u]) for u in units}

    s_cur = {}
    for n in range(GDN_SLOTS):
        sq = n // per_seq
        if n % per_seq == 0:
            s_cur = {h: s_scr[sq, h] for h in range(H)}
        ws = {h: _dot(jnp.concatenate([sol[(n, h)][:, :HD], q_dec[(n, h)]], axis=0).astype(bf16), s_cur[h].astype(bf16))
              for h in range(H)}
        vn16 = {h: (sol[(n, h)][:, HD:] - ws[h][:CHUNK]).astype(bf16) for h in range(H)}
        r = {h: _dot(jnp.concatenate([attn[(n, h)], k_dec[(n, h)].T], axis=0).astype(bf16), vn16[h]) for h in range(H)}
        for h in range(H):
            cols = slice(h * HD, (h + 1) * HD)
            rows = slice(n * CHUNK, (n + 1) * CHUNK)
            s_cur[h] = s_cur[h] * decay[(n, h)] + r[h][CHUNK:]
            o = ws[h][CHUNK:] + r[h][:CHUNK]
            o = o * lax.rsqrt(jnp.mean(o * o, axis=-1, keepdims=True) + EPS) * gw_ref[...]
            o_ref[rows, cols] = (o * _silu(z_ref[rows, cols])).astype(bf16)
        if n % per_seq == per_seq - 1:
            for h in range(H):
                s_scr[sq, h] = s_cur[h]

    @pl.when(pl.program_id(1) == pl.num_programs(1) - 1)
    def _():
        sfin_ref[...] = s_scr[...]


def _gdn_call(p, hist, s0, conv_w, a_log, dt_bias, gdn_norm_w, layer, n_blocks, steps, n_seq):
    br = GDN_SLOTS * CHUNK
    rows = lambda col, width: pl.BlockSpec((br, width), lambda b, c: (b * steps + c, col))
    vec = pl.BlockSpec((None, 1, HD), lambda b, c: (layer, 0, 0))
    return pl.pallas_call(
        functools.partial(_gdn_kernel, n_seq=n_seq),
        grid=(n_blocks, steps),
        in_specs=[
            rows(COL_CONV // CONV_CH, CONV_CH),
            rows(COL_Z // D_GDN, D_GDN),
            rows(COL_AB // HD, HD),
            pl.BlockSpec((n_seq, 8, CONV_CH), lambda b, c: (b, 0, 0)),
            pl.BlockSpec((n_seq, H, HD, HD), lambda b, c: (b, 0, 0, 0)),
            pl.BlockSpec((None, CONV_W, CONV_CH), lambda b, c: (layer, 0, 0)),
            vec, vec, vec,
        ],
        out_specs=[
            pl.BlockSpec((br, D_GDN), lambda b, c: (b * steps + c, 0)),
            pl.BlockSpec((n_seq, H, HD, HD), lambda b, c: (b, 0, 0, 0)),
            pl.BlockSpec((n_seq, 8, CONV_CH), lambda b, c: (b, 0, 0)),
        ],
        out_shape=[
            jax.ShapeDtypeStruct((n_blocks * steps * br, D_GDN), bf16),
            jax.ShapeDtypeStruct((n_blocks * n_seq, H, HD, HD), f32),
            jax.ShapeDtypeStruct((n_blocks * n_seq, 8, CONV_CH), f32),
        ],
        scratch_shapes=[pltpu.VMEM((n_seq, 8 + br // n_seq, CONV_CH), f32), pltpu.VMEM((n_seq, H, HD, HD), f32)],
        compiler_params=_params(("arbitrary", "arbitrary"), 48),
        name="gated_deltanet",
    )(p, p, p, hist, s0, conv_w, a_log, dt_bias, gdn_norm_w)


def _pad_lanes(v):
    return jnp.pad(v, ((0, 0), (0, HD - v.shape[-1]))).reshape(DEPTH, 1, HD)


def kernel(x_prompt, x_sample, cache_k, cache_v, state_gdn, state_conv, c_prompt, c_sample, w_mod, b_mod, norm_g, ffn_up, ffn_down, w_in, w_out, qk_norm, lam_qk, subln_w, rpb_table, conv_w, a_log, dt_bias, gdn_norm_w):
    xp = x_prompt.reshape(P_TOK, D)
    xs = x_sample.reshape(S_TOK, D)
    c_all = jnp.concatenate([c_prompt, c_sample, jnp.zeros((N_CBATCH - BATCH - DEC_BATCH, D), f32)], axis=0)

    w_up16 = ffn_up.astype(bf16)
    w_down16 = ffn_down.astype(bf16)
    w_in16 = jnp.pad(w_in.astype(bf16), ((0, 0), (0, 0), (0, N_PROJ - N_IN)))
    w_out16 = w_out.astype(bf16)
    norm_g4 = norm_g.reshape(DEPTH, 3, 1, D)
    qk_norm2 = jnp.concatenate([qk_norm, qk_norm], axis=-1).reshape(DEPTH, 2, 1, HD)
    gdn_w = gdn_norm_w.reshape(DEPTH, 1, HD)
    a_log_p, dt_bias_p = _pad_lanes(a_log), _pad_lanes(dt_bias)
    ck = cache_k.reshape(DEPTH, DEC_BATCH, PAST * H, HD)
    cv = cache_v.reshape(DEPTH, DEC_BATCH, PAST * H, HD)
    hist_s = jnp.pad(state_conv, ((0, 0), (0, 0), (8 - (CONV_W - 1), 0), (0, 0)))
    hist_p = jnp.zeros((BATCH, 8, CONV_CH), f32)
    s0_p = jnp.zeros((BATCH, H, HD, HD), f32)

    mod = _mod_call(c_all, w_mod, b_mod).reshape(DEPTH, N_CBATCH, N_MOD, D)
    modg_p = jnp.transpose(jnp.repeat(mod[:, :BATCH], SEQ // GROUP, axis=1), (0, 2, 1, 3))
    modg_s = jnp.transpose(mod[:, BATCH:BATCH + DEC_BATCH], (0, 2, 1, 3))

    pbias, sbias = _bias_call(rpb_table)

    gdn_rows = GDN_SLOTS * CHUNK
    outs = {n: [] for n in ("kp", "vp", "sp", "cp", "ks", "vs", "ss", "cs")}
    for l in range(DEPTH):
        lam_init = 0.8 - 0.6 * math.exp(-0.3 * l)
        xp = _ffn_call(xp, modg_p[l], norm_g4, w_up16, w_down16, l, 0, 0)
        xs = _ffn_call(xs, modg_s[l], norm_g4, w_up16, w_down16, l, 0, 0)
        pp = _proj_call(xp, modg_p[l], norm_g4, w_in16, l)
        ps = _proj_call(xs, modg_s[l], norm_g4, w_in16, l)

        oa_p, k_p, v_p = _attn_prompt_call(pp, pbias, lam_qk, qk_norm2, subln_w, l, lam_init)
        oa_s, k_s, v_s = _attn_sample_call(ps, sbias, ck, cv, lam_qk, qk_norm2, subln_w, l, lam_init)
        og_p, s_p, c_p = _gdn_call(pp, hist_p, s0_p, conv_w, a_log_p, dt_bias_p, gdn_w, l, BATCH, SEQ // gdn_rows, 1)
        og_s, s_s, c_s = _gdn_call(ps, hist_s[l], state_gdn[l], conv_w, a_log_p, dt_bias_p, gdn_w, l,
                                   DEC_BATCH // GDN_SLOTS, 1, GDN_SLOTS)

        xp = _out_call(xp, modg_p[l], oa_p, og_p, w_out16, l)
        xs = _out_call(xs, modg_s[l], oa_s, og_s, w_out16, l)
        xp = _ffn_call(xp, modg_p[l], norm_g4, w_up16, w_down16, l, 1, 2)
        xs = _ffn_call(xs, modg_s[l], norm_g4, w_up16, w_down16, l, 1, 2)

        outs["kp"].append(k_p.reshape(BATCH, SEQ, H, HD))
        outs["vp"].append(v_p.reshape(BATCH, SEQ, H, HD))
        outs["sp"].append(s_p)
        outs["cp"].append(c_p[:, 8 - (CONV_W - 1):])
        outs["ks"].append(k_s.reshape(DEC_BATCH, DEC_SEQ, H, HD))
        outs["vs"].append(v_s.reshape(DEC_BATCH, DEC_SEQ, H, HD))
        outs["ss"].append(s_s)
        outs["cs"].append(c_s[:, 8 - (CONV_W - 1):])

    st = {n: jnp.stack(v) for n, v in outs.items()}
    return (xp.reshape(BATCH, SEQ, D), xs.reshape(DEC_BATCH, DEC_SEQ, D),
            st["kp"], st["vp"], st["sp"], st["cp"], st["ks"], st["vs"], st["ss"], st["cs"])
```

```python
import functools
import math

import jax
import jax.numpy as jnp
from jax import lax
from jax.experimental import pallas as pl
from jax.experimental.pallas import tpu as pltpu

f32 = jnp.float32
bf16 = jnp.bfloat16

D = 2048
BATCH = 2
SEQ = 4096
DEPTH = 2
DEC_BATCH = 32
DEC_SEQ = 64
PAST = 2048
CHUNK = 64
H = 8
HD = 128
HALF = 64
D_ATT = H * HD
D_GDN = H * HD
CONV_W = 4
CONV_CH = 3 * D_GDN
D_FF = 5632
NUM_BUCKETS = 32
FAR_BUCKET = 15
FAR_DIST = 91
N_MOD = 9
N_IN = 3 * D_ATT + CONV_CH + D_GDN + 2 * H
EPS = 1e-6

GROUP = 64
P_TOK = BATCH * SEQ
S_TOK = DEC_BATCH * DEC_SEQ
N_CBATCH = 40
N_PROJ = 7680
COL_K, COL_V, COL_CONV, COL_Z, COL_AB = D_ATT, 2 * D_ATT, 3 * D_ATT, 3 * D_ATT + CONV_CH, 3 * D_ATT + CONV_CH + D_GDN

MASKED = -1e30
SUB = 128

FFN_TM, FFN_FC = 512, 512
PROJ_TM, PROJ_TN = 1024, 512
OUT_TM = 512
MOD_TN = 1024
ATT_TQ = 512
ATT_TK = 512
ATT_NG = 4
ATT_LG = 2 * ATT_TQ // ATT_NG
LOG2E = math.log2(math.e)
SATT_TK = 512
GDN_SLOTS = 2


def _params(sem, vmem_mb):
    return pltpu.CompilerParams(dimension_semantics=sem, vmem_limit_bytes=vmem_mb * 1024 * 1024)


def _dot(a, b):
    return jnp.dot(a, b, preferred_element_type=f32)


def _dot_nt(a, b):
    return lax.dot_general(a, b, (((1,), (1,)), ((), ())), preferred_element_type=f32)


def _silu(x):
    return x * jax.nn.sigmoid(x)


def _modulated_norm(x, ng, scale, shift):
    tm = x.shape[0]
    ms = jnp.mean(x * x, axis=-1, keepdims=True)
    y = x * lax.rsqrt(ms + EPS) * ng
    y = y.reshape(tm // GROUP, GROUP, D) * (1.0 + scale[:, None, :]) + shift[:, None, :]
    return y.reshape(tm, D)


def _gated(y, gate):
    tm = y.shape[0]
    return (y.reshape(tm // GROUP, GROUP, D) * gate[:, None, :]).reshape(tm, D)


def _blockdiag_half():
    r = lax.broadcasted_iota(jnp.int32, (HD, HD), 0) // HALF
    c = lax.broadcasted_iota(jnp.int32, (HD, HD), 1) // HALF
    return (r == c).astype(bf16)


def _half_rms(x, w, bd):
    ss = _dot((x * x).astype(bf16), bd) * (1.0 / HALF)
    return x * lax.rsqrt(ss + EPS) * w


def _mod_kernel(c_ref, w_ref, b_ref, o_ref):
    a = _silu(c_ref[...]).astype(bf16)
    o_ref[...] = _dot(a, w_ref[...].astype(bf16)) + b_ref[...]


def _mod_call(c_all, w_mod, b_mod):
    n = N_MOD * D // MOD_TN
    return pl.pallas_call(
        _mod_kernel,
        grid=(DEPTH, n),
        in_specs=[
            pl.BlockSpec((N_CBATCH, D), lambda l, j: (0, 0)),
            pl.BlockSpec((None, D, MOD_TN), lambda l, j: (l, 0, j)),
            pl.BlockSpec((None, 1, MOD_TN), lambda l, j: (l, 0, j)),
        ],
        out_specs=pl.BlockSpec((None, N_CBATCH, MOD_TN), lambda l, j: (l, 0, j)),
        out_shape=jax.ShapeDtypeStruct((DEPTH, N_CBATCH, N_MOD * D), f32),
        compiler_params=_params(("arbitrary", "arbitrary"), 40),
        name="adaln_mod",
    )(c_all, w_mod, b_mod.reshape(DEPTH, 1, N_MOD * D))


def _ffn_kernel(x_ref, xn_ref, shift0_ref, scale0_ref, shiftn_ref, scalen_ref, gate_ref, ng_ref,
                wg_ref, wu_ref, wd_ref, o_ref, h_scr, acc_scr):
    m, k = pl.program_id(0), pl.program_id(1)
    slot = m % 2

    @pl.when((m == 0) & (k == 0))
    def _():
        h_scr[0] = _modulated_norm(x_ref[...], ng_ref[...], scale0_ref[...], shift0_ref[...]).astype(bf16)

    def ff_chunk():
        h = h_scr[slot]
        act = (_silu(_dot(h, wg_ref[...])) * _dot(h, wu_ref[...])).astype(bf16)
        acc_scr[...] = jnp.where(k == 0, 0.0, acc_scr[...]) + _dot(act, wd_ref[...])

    @pl.when(k < pl.num_programs(1) - 1)
    def _():
        ff_chunk()

    @pl.when(k == pl.num_programs(1) - 1)
    def _():
        ff_chunk()
        h_scr[1 - slot] = _modulated_norm(xn_ref[...], ng_ref[...], scalen_ref[...], shiftn_ref[...]).astype(bf16)
        o_ref[...] = x_ref[...] + 0.5 * _gated(acc_scr[...], gate_ref[...])


def _ffn_call(x, modg, norm_g, w_up, w_down, layer, idx, j):
    n_tok = x.shape[0]
    nm = n_tok // FFN_TM
    nk = D_FF // FFN_FC
    gm = FFN_TM // GROUP
    nxt = lambda m: jnp.minimum(m + 1, nm - 1)
    return pl.pallas_call(
        _ffn_kernel,
        grid=(nm, nk),
        in_specs=[
            pl.BlockSpec((FFN_TM, D), lambda m, k: (m, 0)),
            pl.BlockSpec((FFN_TM, D), lambda m, k: (nxt(m), 0)),
            pl.BlockSpec((None, gm, D), lambda m, k: (3 * j, 0, 0)),
            pl.BlockSpec((None, gm, D), lambda m, k: (3 * j + 1, 0, 0)),
            pl.BlockSpec((None, gm, D), lambda m, k: (3 * j, nxt(m), 0)),
            pl.BlockSpec((None, gm, D), lambda m, k: (3 * j + 1, nxt(m), 0)),
            pl.BlockSpec((None, gm, D), lambda m, k: (3 * j + 2, m, 0)),
            pl.BlockSpec((None, None, 1, D), lambda m, k: (layer, j, 0, 0)),
            pl.BlockSpec((None, None, D, FFN_FC), lambda m, k: (layer, idx, 0, k)),
            pl.BlockSpec((None, None, D, FFN_FC), lambda m, k: (layer, idx, 0, nk + k)),
            pl.BlockSpec((None, None, FFN_FC, D), lambda m, k: (layer, idx, k, 0)),
        ],
        out_specs=pl.BlockSpec((FFN_TM, D), lambda m, k: (m, 0)),
        out_shape=jax.ShapeDtypeStruct((n_tok, D), f32),
        scratch_shapes=[pltpu.VMEM((2, FFN_TM, D), bf16), pltpu.VMEM((FFN_TM, D), f32)],
        compiler_params=_params(("arbitrary", "arbitrary"), 56),
        name="swiglu_half_step",
    )(x, x, modg, modg, modg, modg, modg, norm_g, w_up, w_up, w_down)


def _proj_kernel(x_ref, shift_ref, scale_ref, ng_ref, w_ref, o_ref, h_scr):
    @pl.when(pl.program_id(1) == 0)
    def _():
        h_scr[...] = _modulated_norm(x_ref[...], ng_ref[...], scale_ref[...], shift_ref[...]).astype(bf16)

    o_ref[...] = _dot(h_scr[...], w_ref[...])


def _proj_call(x, modg, norm_g, w_in, layer):
    n_tok = x.shape[0]
    gm = PROJ_TM // GROUP
    row = lambda r: pl.BlockSpec((None, gm, D), lambda m, n: (r, m, 0))
    return pl.pallas_call(
        _proj_kernel,
        grid=(n_tok // PROJ_TM, N_PROJ // PROJ_TN),
        in_specs=[
            pl.BlockSpec((PROJ_TM, D), lambda m, n: (m, 0)),
            row(3), row(4),
            pl.BlockSpec((None, None, 1, D), lambda m, n: (layer, 1, 0, 0)),
            pl.BlockSpec((None, D, PROJ_TN), lambda m, n: (layer, 0, n)),
        ],
        out_specs=pl.BlockSpec((PROJ_TM, PROJ_TN), lambda m, n: (m, n)),
        out_shape=jax.ShapeDtypeStruct((n_tok, N_PROJ), f32),
        scratch_shapes=[pltpu.VMEM((PROJ_TM, D), bf16)],
        compiler_params=_params(("arbitrary", "arbitrary"), 40),
        name="mixer_in_proj",
    )(x, modg, modg, norm_g, w_in)


def _out_kernel(x_ref, gate_ref, a_ref, g_ref, w_ref, o_ref):
    y = _dot(a_ref[...], w_ref[:D_ATT, :]) + _dot(g_ref[...], w_ref[D_ATT:, :])
    o_ref[...] = x_ref[...] + _gated(y, gate_ref[...])


def _out_call(x, modg, o_att, o_gdn, w_out, layer):
    n_tok = x.shape[0]
    gm = OUT_TM // GROUP
    return pl.pallas_call(
        _out_kernel,
        grid=(n_tok // OUT_TM,),
        in_specs=[
            pl.BlockSpec((OUT_TM, D), lambda m: (m, 0)),
            pl.BlockSpec((None, gm, D), lambda m: (5, m, 0)),
            pl.BlockSpec((OUT_TM, D_ATT), lambda m: (m, 0)),
            pl.BlockSpec((OUT_TM, D_GDN), lambda m: (m, 0)),
            pl.BlockSpec((None, D, D), lambda m: (layer, 0, 0)),
        ],
        out_specs=pl.BlockSpec((OUT_TM, D), lambda m: (m, 0)),
        out_shape=jax.ShapeDtypeStruct((n_tok, D), f32),
        compiler_params=_params(("arbitrary",), 48),
        name="mixer_out_proj",
    )(x, modg, o_att, o_gdn, w_out)


def _bias_from_rel(rel, table_ref, h):
    n = jnp.abs(rel)
    big = jnp.full_like(n, 8)
    for t in (12, 16, 23, 32, 46, 64, FAR_DIST):
        big = big + (n >= t).astype(jnp.int32)
    bucket = jnp.where(n < 8, n, big) + jnp.where(rel > 0, NUM_BUCKETS // 2, 0)
    out = jnp.zeros(rel.shape, f32)
    for b in range(NUM_BUCKETS):
        out = jnp.where(bucket == b, table_ref[b, h], out)
    return out


def _fill_bias(dst_ref, lead, n_rows, n_cols, col0, key_axis, key_base, table_ref, h, scale=1.0):
    far = table_ref[FAR_BUCKET, h]
    for r0 in range(0, n_rows, SUB):
        nr = min(SUB, n_rows - r0)
        for c0 in range(0, n_cols, SUB):
            r = lax.broadcasted_iota(jnp.int32, (nr, SUB), 0) + r0
            c = lax.broadcasted_iota(jnp.int32, (nr, SUB), 1) + c0
            key, query = (r + key_base, c) if key_axis == 0 else (c + key_base, r)
            k_lo, q_lo = (r0 + key_base, c0) if key_axis == 0 else (c0 + key_base, r0)
            k_hi = k_lo + (nr if key_axis == 0 else SUB) - 1
            q_hi = q_lo + (SUB if key_axis == 0 else nr) - 1
            if k_lo // CHUNK > q_hi // CHUNK:
                tile = jnp.full((nr, SUB), MASKED, f32)
            elif k_hi - q_lo <= -FAR_DIST:
                tile = jnp.zeros((nr, SUB), f32)
            else:
                tile = (_bias_from_rel(key - query, table_ref, h) - far) * scale
                tile = jnp.where(key // CHUNK <= query // CHUNK, tile, MASKED)
            dst_ref[lead + (slice(r0, r0 + nr), slice(col0 + c0, col0 + c0 + SUB))] = tile


def _bias_kernel(table_ref, pb_ref, sb_ref):
    h = pl.program_id(0)
    _fill_bias(pb_ref, (0,), 2 * ATT_TQ, ATT_TQ, 0, 0, 0, table_ref, h, LOG2E)
    _fill_bias(pb_ref, (1,), 2 * ATT_TQ, ATT_TQ, 0, 0, -ATT_TQ, table_ref, h, LOG2E)
    _fill_bias(sb_ref, (), DEC_SEQ, SATT_TK, 0, 1, -SATT_TK, table_ref, h)
    _fill_bias(sb_ref, (), DEC_SEQ, HD, SATT_TK, 1, 0, table_ref, h)


def _bias_call(rpb_table):
    return pl.pallas_call(
        _bias_kernel,
        grid=(H,),
        in_specs=[pl.BlockSpec(memory_space=pltpu.SMEM)],
        out_specs=[
            pl.BlockSpec((None, 2, 2 * ATT_TQ, ATT_TQ), lambda h: (h, 0, 0, 0)),
            pl.BlockSpec((None, DEC_SEQ, SATT_TK + HD), lambda h: (h, 0, 0)),
        ],
        out_shape=[
            jax.ShapeDtypeStruct((H, 2, 2 * ATT_TQ, ATT_TQ), f32),
            jax.ShapeDtypeStruct((H, DEC_SEQ, SATT_TK + HD), f32),
        ],
        compiler_params=_params(("arbitrary",), 32),
        name="rel_pos_bias_tiles",
    )(rpb_table)


def _two_map_queries(qn):
    lane = lax.broadcasted_iota(jnp.int32, qn.shape, 1)
    return jnp.concatenate([jnp.where(lane < HALF, qn, 0.0), jnp.where(lane >= HALF, qn, 0.0)], axis=0).astype(bf16)


def _add_map_bias(s, b):
    n = b.shape[0]
    return (s.reshape(2, n, s.shape[1]) + b[None]).reshape(s.shape)


def _diff_combine(l, acc, lam_ref, sw, lam_init):
    n = acc.shape[0] // 2
    lq = lam_ref[...]
    lam = (jnp.exp(jnp.sum(lq[0:1] * lq[1:2], axis=-1, keepdims=True))
           - jnp.exp(jnp.sum(lq[2:3] * lq[3:4], axis=-1, keepdims=True)) + lam_init)
    o = acc[:n] / l[:n] - lam * (acc[n:] / l[n:])
    o = o * lax.rsqrt(jnp.mean(o * o, axis=-1, keepdims=True) + EPS) * sw
    return o * (1.0 - lam_init)


def _attn_prompt_kernel(lam_ref, qw_ref, kw_ref, sw_ref, bias_ref, q_ref, k_ref, v_ref,
                        o_ref, knew_ref, vnew_ref, kn_scr, vt_scr, *, lam_init):
    qi = pl.program_id(2)
    bd = _blockdiag_half()

    @pl.when(qi == 0)
    def _():
        kn = _half_rms(k_ref[...], kw_ref[...], bd)
        knew_ref[...] = kn
        kn_scr[...] = kn.astype(bf16)
        for c in range(SEQ // ATT_TK):
            v = v_ref[c * ATT_TK:(c + 1) * ATT_TK, :]
            vnew_ref[c * ATT_TK:(c + 1) * ATT_TK, :] = v
            vt_scr[c] = v.T.astype(bf16)

    qt = (_half_rms(q_ref[...], qw_ref[...], bd) * (HALF ** -0.5 * LOG2E)).T
    feat = lax.broadcasted_iota(jnp.int32, qt.shape, 0)
    q2t = jnp.concatenate([jnp.where(feat < HALF, qt, 0.0), jnp.where(feat >= HALF, qt, 0.0)], axis=1).astype(bf16)
    q_groups = [q2t[:, g * ATT_LG:(g + 1) * ATT_LG] for g in range(ATT_NG)]

    def logits(j, groups):
        keys = kn_scr[pl.ds(pl.multiple_of(j * ATT_TK, ATT_TK), ATT_TK), :]
        return [_dot(keys, q_groups[g]) for g in groups]

    def consume(j, g, s, state, bias):
        m, l, acc, p_prev = state
        acc = acc + _dot(vt_scr[jnp.maximum(j - 1, 0)], p_prev)
        if bias is not None:
            q0 = (g * ATT_LG) % ATT_TQ
            s = s + bias[:, q0:q0 + ATT_LG]
        m_new = jnp.maximum(m, jnp.max(s, axis=0, keepdims=True))
        p = jnp.exp2(s - m_new)
        alpha = jnp.exp2(m - m_new)
        return m_new, alpha * l + jnp.sum(p, axis=0, keepdims=True), alpha * acc, p.astype(bf16)

    def step(j, states, bias=None):
        s = logits(j, range(ATT_NG))
        return [consume(j, g, s[g], states[g], bias) for g in range(ATT_NG)]

    states = [(jnp.full((1, ATT_LG), MASKED, f32), jnp.zeros((1, ATT_LG), f32), jnp.zeros((HD, ATT_LG), f32),
               jnp.zeros((ATT_TK, ATT_LG), bf16)) for _ in range(ATT_NG)]
    n_near = 2 * ATT_TQ // ATT_TK
    near0 = jnp.maximum(qi - 1, 0) * (ATT_TQ // ATT_TK)
    states = lax.fori_loop(0, near0, step, states)
    variant = jnp.minimum(qi, 1)
    for part in range(n_near):
        states = step(near0 + part, states, bias_ref[variant, part * ATT_TK:(part + 1) * ATT_TK, :])
    states = [(m, l, acc + _dot(vt_scr[near0 + n_near - 1], p)) for m, l, acc, p in states]
    l = jnp.concatenate([st[1] for st in states], axis=1)
    acc = jnp.concatenate([st[2] for st in states], axis=1)

    lq = lam_ref[...]
    lam = (jnp.exp(jnp.sum(lq[0:1] * lq[1:2], axis=-1, keepdims=True))
           - jnp.exp(jnp.sum(lq[2:3] * lq[3:4], axis=-1, keepdims=True)) + lam_init)
    o2 = acc / l
    o = (o2[:, :ATT_TQ] - lam * o2[:, ATT_TQ:]).T
    o = o * lax.rsqrt(jnp.mean(o * o, axis=-1, keepdims=True) + EPS) * sw_ref[...]
    o_ref[...] = (o * (1.0 - lam_init)).astype(bf16)


def _attn_prompt_call(p, bias, lam_qk, qk_norm2, subln_w, layer, lam_init):
    nq = SEQ // ATT_TQ
    vec = lambda r: pl.BlockSpec((None, None, 1, HD), lambda b, h, i: (layer, r, 0, 0))
    head_seq = pl.BlockSpec((SEQ, HD), lambda b, h, i: (b, h))
    return pl.pallas_call(
        functools.partial(_attn_prompt_kernel, lam_init=lam_init),
        grid=(BATCH, H, nq),
        in_specs=[
            pl.BlockSpec((None, 4, HALF), lambda b, h, i: (layer, 0, 0)),
            vec(0), vec(1),
            pl.BlockSpec((None, 1, HD), lambda b, h, i: (layer, 0, 0)),
            pl.BlockSpec((None, 2, 2 * ATT_TQ, ATT_TQ), lambda b, h, i: (h, 0, 0, 0)),
            pl.BlockSpec((ATT_TQ, HD), lambda b, h, i: (b * nq + i, h)),
            pl.BlockSpec((SEQ, HD), lambda b, h, i: (b, H + h)),
            pl.BlockSpec((SEQ, HD), lambda b, h, i: (b, 2 * H + h)),
        ],
        out_specs=[pl.BlockSpec((ATT_TQ, HD), lambda b, h, i: (b * nq + i, h)), head_seq, head_seq],
        out_shape=[
            jax.ShapeDtypeStruct((P_TOK, D_ATT), bf16),
            jax.ShapeDtypeStruct((P_TOK, D_ATT), f32),
            jax.ShapeDtypeStruct((P_TOK, D_ATT), f32),
        ],
        scratch_shapes=[pltpu.VMEM((SEQ, HD), bf16), pltpu.VMEM((SEQ // ATT_TK, HD, ATT_TK), bf16)],
        compiler_params=_params(("arbitrary", "arbitrary", "arbitrary"), 56),
        name="diff_attn_prompt",
    )(lam_qk, qk_norm2, qk_norm2, subln_w.reshape(DEPTH, 1, HD), bias, p, p, p)


def _attn_sample_kernel(lam_ref, qw_ref, kw_ref, sw_ref, bias_ref, q_ref, kx_ref, vx_ref, ck_ref, cv_ref,
                        o_ref, knew_ref, vnew_ref, *, lam_init):
    bd = _blockdiag_half()
    n_blk = PAST // SATT_TK
    heads, blocks = range(H), range(n_blk)
    cols = [slice(h * HD, (h + 1) * HD) for h in heads]
    vnew_ref[...] = vx_ref[...]
    q2 = [_two_map_queries(_half_rms(q_ref[:, cols[h]], qw_ref[...], bd) * (HALF ** -0.5)) for h in heads]
    kn = [_half_rms(kx_ref[:, cols[h]], kw_ref[...], bd) for h in heads]
    for h in heads:
        knew_ref[:, cols[h]] = kn[h]

    def cached(ref, h, i):
        return ref[pl.ds(i * SATT_TK * H + h, SATT_TK, stride=H), :].astype(bf16)

    ss = [[_dot_nt(q2[h], cached(ck_ref, h, i)) for i in blocks] for h in heads]
    for h in heads:
        ss[h][-1] = _add_map_bias(ss[h][-1], bias_ref[h, :, :SATT_TK])
        ss[h].append(_add_map_bias(_dot_nt(q2[h], kn[h].astype(bf16)), bias_ref[h, :, SATT_TK:SATT_TK + DEC_SEQ]))
    for h in heads:
        m = functools.reduce(jnp.maximum, [jnp.max(s, axis=-1, keepdims=True) for s in ss[h]])
        ps = [jnp.exp(s - m) for s in ss[h]]
        l = functools.reduce(jnp.add, [jnp.sum(p, axis=-1, keepdims=True) for p in ps])
        vals = [cached(cv_ref, h, i) for i in blocks] + [vx_ref[:, cols[h]].astype(bf16)]
        acc = functools.reduce(jnp.add, [_dot(p.astype(bf16), v) for p, v in zip(ps, vals)])
        o_ref[:, cols[h]] = _diff_combine(l, acc, lam_ref, sw_ref[...], lam_init).astype(bf16)


def _attn_sample_call(p, bias, cache_k, cache_v, lam_qk, qk_norm2, subln_w, layer, lam_init):
    vec = lambda r: pl.BlockSpec((None, None, 1, HD), lambda b: (layer, r, 0, 0))
    sec = lambda c: pl.BlockSpec((DEC_SEQ, D_ATT), lambda b: (b, c))
    cache = pl.BlockSpec((None, None, PAST * H, HD), lambda b: (layer, b, 0, 0))
    stream = pl.BlockSpec((DEC_SEQ, D_ATT), lambda b: (b, 0))
    return pl.pallas_call(
        functools.partial(_attn_sample_kernel, lam_init=lam_init),
        grid=(DEC_BATCH,),
        in_specs=[
            pl.BlockSpec((None, 4, HALF), lambda b: (layer, 0, 0)),
            vec(0), vec(1),
            pl.BlockSpec((None, 1, HD), lambda b: (layer, 0, 0)),
            pl.BlockSpec((H, DEC_SEQ, SATT_TK + HD), lambda b: (0, 0, 0)),
            sec(0), sec(1), sec(2), cache, cache,
        ],
        out_specs=[stream, stream, stream],
        out_shape=[
            jax.ShapeDtypeStruct((S_TOK, D_ATT), bf16),
            jax.ShapeDtypeStruct((S_TOK, D_ATT), f32),
            jax.ShapeDtypeStruct((S_TOK, D_ATT), f32),
        ],
        compiler_params=_params(("arbitrary",), 60),
        name="diff_attn_sample",
    )(lam_qk, qk_norm2, qk_norm2, subln_w.reshape(DEPTH, 1, HD), bias, p, p, p, cache_k, cache_v)


def _gdn_kernel(p_ref, z_ref, ab_ref, hist_ref, s0_ref, cw_ref, alog_ref, dtb_ref, gw_ref,
                o_ref, sfin_ref, cnew_ref, xp_scr, s_scr, *, n_seq):
    per_seq = GDN_SLOTS // n_seq
    rows_seq = per_seq * CHUNK

    @pl.when(pl.program_id(1) == 0)
    def _():
        xp_scr[:, 0:8, :] = hist_ref[...]
        s_scr[...] = s0_ref[...]

    ys = []
    for sq in range(n_seq):
        xp_scr[sq, 8:8 + rows_seq, :] = p_ref[sq * rows_seq:(sq + 1) * rows_seq, :]
        y = xp_scr[sq, 5:5 + rows_seq, :] * cw_ref[0:1, :]
        for i in range(1, CONV_W):
            y = y + xp_scr[sq, 5 + i:5 + i + rows_seq, :] * cw_ref[i:i + 1, :]
        y = _silu(y)
        tail = xp_scr[sq, rows_seq:rows_seq + 8, :]
        xp_scr[sq, 0:8, :] = tail
        cnew_ref[sq] = tail
        ys += [y[c * CHUNK:(c + 1) * CHUNK] for c in range(per_seq)]

    ii = lax.broadcasted_iota(jnp.int32, (CHUNK, CHUNK), 0)
    jj = lax.broadcasted_iota(jnp.int32, (CHUNK, CHUNK), 1)
    causal = ii >= jj
    strict = ii > jj
    eye = jnp.where(ii == jj, 1.0, 0.0)

    units = [(n, h) for n in range(GDN_SLOTS) for h in range(H)]
    q16, k16, kb16, rhs16, gamma, q_dec, k_dec, decay = {}, {}, {}, {}, {}, {}, {}, {}
    for n in range(GDN_SLOTS):
        y = ys[n]
        ab = ab_ref[n * CHUNK:(n + 1) * CHUNK, :]
        pre = ab + dtb_ref[...]
        softplus = jnp.maximum(pre, 0.0) + jnp.log1p(jnp.exp(-jnp.abs(pre)))
        g_all = -jnp.exp(alog_ref[...]) * softplus
        beta_all = jax.nn.sigmoid(ab)
        for h in range(H):
            u = (n, h)
            qx = y[:, h * HD:(h + 1) * HD]
            kx = y[:, D_GDN + h * HD:D_GDN + (h + 1) * HD]
            v = y[:, 2 * D_GDN + h * HD:2 * D_GDN + (h + 1) * HD]
            q = qx * lax.rsqrt(jnp.sum(qx * qx, axis=-1, keepdims=True) + EPS) * (HD ** -0.5)
            k = kx * lax.rsqrt(jnp.sum(kx * kx, axis=-1, keepdims=True) + EPS)
            g = g_all[:, h:h + 1]
            beta = beta_all[:, H + h:H + h + 1]
            gc_row = jnp.sum(jnp.where(ii <= jj, g, 0.0), axis=0, keepdims=True)
            gc = jnp.sum(jnp.where(ii == jj, gc_row, 0.0), axis=1, keepdims=True)
            gc_last = jnp.sum(g, axis=0, keepdims=True)
            gamma[u] = jnp.where(causal, jnp.exp(jnp.where(causal, gc - gc_row, 0.0)), 0.0)
            egc = jnp.exp(gc)
            kb = k * beta
            q16[u], k16[u], kb16[u] = q.astype(bf16), k.astype(bf16), kb.astype(bf16)
            rhs16[u] = jnp.concatenate([kb * egc, v * beta], axis=-1).astype(bf16)
            q_dec[u] = q * egc
            k_dec[u] = k * jnp.exp(gc_last - gc)
            decay[u] = jnp.exp(gc_last)

    kk = {u: _dot_nt(kb16[u], k16[u]) for u in units}
    qk = {u: _dot_nt(q16[u], k16[u]) for u in units}
    a = {u: jnp.where(strict, kk[u] * gamma[u], 0.0) for u in units}
    attn = {u: qk[u] * gamma[u] for u in units}

    base = 8
    d = {u: jnp.where(ii // base == jj // base, a[u], 0.0) for u in units}
    t = {u: eye - d[u] for u in units}
    d16 = {u: d[u].astype(bf16) for u in units}
    pw16 = {u: _dot(d16[u], d16[u]).astype(bf16) for u in units}
    t = {u: t[u] + _dot(t[u].astype(bf16), pw16[u]) for u in units}
    pw16 = {u: _dot(pw16[u], pw16[u]).astype(bf16) for u in units}
    t = {u: t[u] + _dot(t[u].astype(bf16), pw16[u]) for u in units}
    size = base
    while size < CHUNK:
        off = (ii // (2 * size) == jj // (2 * size)) & (ii // size != jj // size)
        t16 = {u: t[u].astype(bf16) for u in units}
        ta = {u: _dot(t16[u], jnp.where(off, a[u], 0.0).astype(bf16)) for u in units}
        t = {u: t[u] - _dot(ta[u].astype(bf16), t16[u]) for u in units}
        size *= 2
    sol = {u: _dot(t[u].astype(bf16), rhs16[u]) for u in units}

    s_cur = {}
    for n in range(GDN_SLOTS):
        sq = n // per_seq
        if n % per_seq == 0:
            s_cur = {h: s_scr[sq, h] for h in range(H)}
        ws = {h: _dot(jnp.concatenate([sol[(n, h)][:, :HD], q_dec[(n, h)]], axis=0).astype(bf16), s_cur[h].astype(bf16))
              for h in range(H)}
        vn16 = {h: (sol[(n, h)][:, HD:] - ws[h][:CHUNK]).astype(bf16) for h in range(H)}
        r = {h: _dot(jnp.concatenate([attn[(n, h)], k_dec[(n, h)].T], axis=0).astype(bf16), vn16[h]) for h in range(H)}
        for h in range(H):
            cols = slice(h * HD, (h + 1) * HD)
            rows = slice(n * CHUNK, (n + 1) * CHUNK)
            s_cur[h] = s_cur[h] * decay[(n, h)] + r[h][CHUNK:]
            o = ws[h][CHUNK:] + r[h][:CHUNK]
            o = o * lax.rsqrt(jnp.mean(o * o, axis=-1, keepdims=True) + EPS) * gw_ref[...]
            o_ref[rows, cols] = (o * _silu(z_ref[rows, cols])).astype(bf16)
        if n % per_seq == per_seq - 1:
            for h in range(H):
                s_scr[sq, h] = s_cur[h]

    @pl.when(pl.program_id(1) == pl.num_programs(1) - 1)
    def _():
        sfin_ref[...] = s_scr[...]


def _gdn_call(p, hist, s0, conv_w, a_log, dt_bias, gdn_norm_w, layer, n_blocks, steps, n_seq):
    br = GDN_SLOTS * CHUNK
    rows = lambda col, width: pl.BlockSpec((br, width), lambda b, c: (b * steps + c, col))
    vec = pl.BlockSpec((None, 1, HD), lambda b, c: (layer, 0, 0))
    return pl.pallas_call(
        functools.partial(_gdn_kernel, n_seq=n_seq),
        grid=(n_blocks, steps),
        in_specs=[
            rows(COL_CONV // CONV_CH, CONV_CH),
            rows(COL_Z // D_GDN, D_GDN),
            rows(COL_AB // HD, HD),
            pl.BlockSpec((n_seq, 8, CONV_CH), lambda b, c: (b, 0, 0)),
            pl.BlockSpec((n_seq, H, HD, HD), lambda b, c: (b, 0, 0, 0)),
            pl.BlockSpec((None, CONV_W, CONV_CH), lambda b, c: (layer, 0, 0)),
            vec, vec, vec,
        ],
        out_specs=[
            pl.BlockSpec((br, D_GDN), lambda b, c: (b * steps + c, 0)),
            pl.BlockSpec((n_seq, H, HD, HD), lambda b, c: (b, 0, 0, 0)),
            pl.BlockSpec((n_seq, 8, CONV_CH), lambda b, c: (b, 0, 0)),
        ],
        out_shape=[
            jax.ShapeDtypeStruct((n_blocks * steps * br, D_GDN), bf16),
            jax.ShapeDtypeStruct((n_blocks * n_seq, H, HD, HD), f32),
            jax.ShapeDtypeStruct((n_blocks * n_seq, 8, CONV_CH), f32),
        ],
        scratch_shapes=[pltpu.VMEM((n_seq, 8 + br // n_seq, CONV_CH), f32), pltpu.VMEM((n_seq, H, HD, HD), f32)],
        compiler_params=_params(("arbitrary", "arbitrary"), 48),
        name="gated_deltanet",
    )(p, p, p, hist, s0, conv_w, a_log, dt_bias, gdn_norm_w)


def _pad_lanes(v):
    return jnp.pad(v, ((0, 0), (0, HD - v.shape[-1]))).reshape(DEPTH, 1, HD)


def kernel(x_prompt, x_sample, cache_k, cache_v, state_gdn, state_conv, c_prompt, c_sample, w_mod, b_mod, norm_g, ffn_up, ffn_down, w_in, w_out, qk_norm, lam_qk, subln_w, rpb_table, conv_w, a_log, dt_bias, gdn_norm_w):
    xp = x_prompt.reshape(P_TOK, D)
    xs = x_sample.reshape(S_TOK, D)
    c_all = jnp.concatenate([c_prompt, c_sample, jnp.zeros((N_CBATCH - BATCH - DEC_BATCH, D), f32)], axis=0)

    w_up16 = ffn_up.astype(bf16)
    w_down16 = ffn_down.astype(bf16)
    w_in16 = jnp.pad(w_in.astype(bf16), ((0, 0), (0, 0), (0, N_PROJ - N_IN)))
    w_out16 = w_out.astype(bf16)
    norm_g4 = norm_g.reshape(DEPTH, 3, 1, D)
    qk_norm2 = jnp.concatenate([qk_norm, qk_norm], axis=-1).reshape(DEPTH, 2, 1, HD)
    gdn_w = gdn_norm_w.reshape(DEPTH, 1, HD)
    a_log_p, dt_bias_p = _pad_lanes(a_log), _pad_lanes(dt_bias)
    ck = cache_k.reshape(DEPTH, DEC_BATCH, PAST * H, HD)
    cv = cache_v.reshape(DEPTH, DEC_BATCH, PAST * H, HD)
    hist_s = jnp.pad(state_conv, ((0, 0), (0, 0), (8 - (CONV_W - 1), 0), (0, 0)))
    hist_p = jnp.zeros((BATCH, 8, CONV_CH), f32)
    s0_p = jnp.zeros((BATCH, H, HD, HD), f32)

    mod = _mod_call(c_all, w_mod, b_mod).reshape(DEPTH, N_CBATCH, N_MOD, D)
    modg_p = jnp.transpose(jnp.repeat(mod[:, :BATCH], SEQ // GROUP, axis=1), (0, 2, 1, 3))
    modg_s = jnp.transpose(mod[:, BATCH:BATCH + DEC_BATCH], (0, 2, 1, 3))

    pbias, sbias = _bias_call(rpb_table)

    gdn_rows = GDN_SLOTS * CHUNK
    outs = {n: [] for n in ("kp", "vp", "sp", "cp", "ks", "vs", "ss", "cs")}
    for l in range(DEPTH):
        lam_init = 0.8 - 0.6 * math.exp(-0.3 * l)
        xp = _ffn_call(xp, modg_p[l], norm_g4, w_up16, w_down16, l, 0, 0)
        xs = _ffn_call(xs, modg_s[l], norm_g4, w_up16, w_down16, l, 0, 0)
        pp = _proj_call(xp, modg_p[l], norm_g4, w_in16, l)
        ps = _proj_call(xs, modg_s[l], norm_g4, w_in16, l)

        oa_p, k_p, v_p = _attn_prompt_call(pp, pbias, lam_qk, qk_norm2, subln_w, l, lam_init)
        oa_s, k_s, v_s = _attn_sample_call(ps, sbias, ck, cv, lam_qk, qk_norm2, subln_w, l, lam_init)
        og_p, s_p, c_p = _gdn_call(pp, hist_p, s0_p, conv_w, a_log_p, dt_bias_p, gdn_w, l, BATCH, SEQ // gdn_rows, 1)
        og_s, s_s, c_s = _gdn_call(ps, hist_s[l], state_gdn[l], conv_w, a_log_p, dt_bias_p, gdn_w, l,
                                   DEC_BATCH // GDN_SLOTS, 1, GDN_SLOTS)

        xp = _out_call(xp, modg_p[l], oa_p, og_p, w_out16, l)
        xs = _out_call(xs, modg_s[l], oa_s, og_s, w_out16, l)
        xp = _ffn_call(xp, modg_p[l], norm_g4, w_up16, w_down16, l, 1, 2)
        xs = _ffn_call(xs, modg_s[l], norm_g4, w_up16, w_down16, l, 1, 2)

        outs["kp"].append(k_p.reshape(BATCH, SEQ, H, HD))
        outs["vp"].append(v_p.reshape(BATCH, SEQ, H, HD))
        outs["sp"].append(s_p)
        outs["cp"].append(c_p[:, 8 - (CONV_W - 1):])
        outs["ks"].append(k_s.reshape(DEC_BATCH, DEC_SEQ, H, HD))
        outs["vs"].append(v_s.reshape(DEC_BATCH, DEC_SEQ, H, HD))
        outs["ss"].append(s_s)
        outs["cs"].append(c_s[:, 8 - (CONV_W - 1):])

    st = {n: jnp.stack(v) for n, v in outs.items()}
    return (xp.reshape(BATCH, SEQ, D), xs.reshape(DEC_BATCH, DEC_SEQ, D),
            st["kp"], st["vp"], st["sp"], st["cp"], st["ks"], st["vs"], st["ss"], st["cs"])
```

```python
import functools
import math

import jax
import jax.numpy as jnp
from jax import lax
from jax.experimental import pallas as pl
from jax.experimental.pallas import tpu as pltpu

f32 = jnp.float32
bf16 = jnp.bfloat16

D = 2048
BATCH = 2
SEQ = 4096
DEPTH = 2
DEC_BATCH = 32
DEC_SEQ = 64
PAST = 2048
CHUNK = 64
H = 8
HD = 128
HALF = 64
D_ATT = H * HD
D_GDN = H * HD
CONV_W = 4
CONV_CH = 3 * D_GDN
D_FF = 5632
NUM_BUCKETS = 32
FAR_BUCKET = 15
FAR_DIST = 91
N_MOD = 9
N_IN = 3 * D_ATT + CONV_CH + D_GDN + 2 * H
EPS = 1e-6

GROUP = 64
P_TOK = BATCH * SEQ
S_TOK = DEC_BATCH * DEC_SEQ
N_CBATCH = 40
N_PROJ = 7680
COL_K, COL_V, COL_CONV, COL_Z, COL_AB = D_ATT, 2 * D_ATT, 3 * D_ATT, 3 * D_ATT + CONV_CH, 3 * D_ATT + CONV_CH + D_GDN

MASKED = -1e30
SUB = 128

FFN_TM, FFN_FC = 512, 512
PROJ_TM, PROJ_TN = 1024, 512
OUT_TM = 512
MOD_TN = 1024
ATT_TQ = 512
ATT_TK = 512
ATT_NG = 4
ATT_LG = 2 * ATT_TQ // ATT_NG
LOG2E = math.log2(math.e)
SATT_TK = 512
GDN_SLOTS = 2


def _params(sem, vmem_mb):
    return pltpu.CompilerParams(dimension_semantics=sem, vmem_limit_bytes=vmem_mb * 1024 * 1024)


def _dot(a, b):
    return jnp.dot(a, b, preferred_element_type=f32)


def _dot_nt(a, b):
    return lax.dot_general(a, b, (((1,), (1,)), ((), ())), preferred_element_type=f32)


def _silu(x):
    return x * jax.nn.sigmoid(x)


def _modulated_norm(x, ng, scale, shift):
    tm = x.shape[0]
    ms = jnp.mean(x * x, axis=-1, keepdims=True)
    y = x * lax.rsqrt(ms + EPS) * ng
    y = y.reshape(tm // GROUP, GROUP, D) * (1.0 + scale[:, None, :]) + shift[:, None, :]
    return y.reshape(tm, D)


def _gated(y, gate):
    tm = y.shape[0]
    return (y.reshape(tm // GROUP, GROUP, D) * gate[:, None, :]).reshape(tm, D)


def _blockdiag_half():
    r = lax.broadcasted_iota(jnp.int32, (HD, HD), 0) // HALF
    c = lax.broadcasted_iota(jnp.int32, (HD, HD), 1) // HALF
    return (r == c).astype(bf16)


def _half_rms(x, w, bd):
    ss = _dot((x * x).astype(bf16), bd) * (1.0 / HALF)
    return x * lax.rsqrt(ss + EPS) * w


def _mod_kernel(c_ref, w_ref, b_ref, o_ref):
    a = _silu(c_ref[...]).astype(bf16)
    o_ref[...] = _dot(a, w_ref[...].astype(bf16)) + b_ref[...]


def _mod_call(c_all, w_mod, b_mod):
    n = N_MOD * D // MOD_TN
    return pl.pallas_call(
        _mod_kernel,
        grid=(DEPTH, n),
        in_specs=[
            pl.BlockSpec((N_CBATCH, D), lambda l, j: (0, 0)),
            pl.BlockSpec((None, D, MOD_TN), lambda l, j: (l, 0, j)),
            pl.BlockSpec((None, 1, MOD_TN), lambda l, j: (l, 0, j)),
        ],
        out_specs=pl.BlockSpec((None, N_CBATCH, MOD_TN), lambda l, j: (l, 0, j)),
        out_shape=jax.ShapeDtypeStruct((DEPTH, N_CBATCH, N_MOD * D), f32),
        compiler_params=_params(("arbitrary", "arbitrary"), 40),
        name="adaln_mod",
    )(c_all, w_mod, b_mod.reshape(DEPTH, 1, N_MOD * D))


def _ffn_kernel(x_ref, xn_ref, shift0_ref, scale0_ref, shiftn_ref, scalen_ref, gate_ref, ng_ref,
                wg_ref, wu_ref, wd_ref, o_ref, h_scr, acc_scr):
    m, k = pl.program_id(0), pl.program_id(1)
    slot = m % 2

    @pl.when((m == 0) & (k == 0))
    def _():
        h_scr[0] = _modulated_norm(x_ref[...], ng_ref[...], scale0_ref[...], shift0_ref[...]).astype(bf16)

    def ff_chunk():
        h = h_scr[slot]
        act = (_silu(_dot(h, wg_ref[...])) * _dot(h, wu_ref[...])).astype(bf16)
        acc_scr[...] = jnp.where(k == 0, 0.0, acc_scr[...]) + _dot(act, wd_ref[...])

    @pl.when(k < pl.num_programs(1) - 1)
    def _():
        ff_chunk()

    @pl.when(k == pl.num_programs(1) - 1)
    def _():
        ff_chunk()
        h_scr[1 - slot] = _modulated_norm(xn_ref[...], ng_ref[...], scalen_ref[...], shiftn_ref[...]).astype(bf16)
        o_ref[...] = x_ref[...] + 0.5 * _gated(acc_scr[...], gate_ref[...])


def _ffn_call(x, modg, norm_g, w_up, w_down, layer, idx, j):
    n_tok = x.shape[0]
    nm = n_tok // FFN_TM
    nk = D_FF // FFN_FC
    gm = FFN_TM // GROUP
    nxt = lambda m: jnp.minimum(m + 1, nm - 1)
    return pl.pallas_call(
        _ffn_kernel,
        grid=(nm, nk),
        in_specs=[
            pl.BlockSpec((FFN_TM, D), lambda m, k: (m, 0)),
            pl.BlockSpec((FFN_TM, D), lambda m, k: (nxt(m), 0)),
            pl.BlockSpec((None, gm, D), lambda m, k: (3 * j, 0, 0)),
            pl.BlockSpec((None, gm, D), lambda m, k: (3 * j + 1, 0, 0)),
            pl.BlockSpec((None, gm, D), lambda m, k: (3 * j, nxt(m), 0)),
            pl.BlockSpec((None, gm, D), lambda m, k: (3 * j + 1, nxt(m), 0)),
            pl.BlockSpec((None, gm, D), lambda m, k: (3 * j + 2, m, 0)),
            pl.BlockSpec((None, None, 1, D), lambda m, k: (layer, j, 0, 0)),
            pl.BlockSpec((None, None, None, D, FFN_FC), lambda m, k: (layer, idx, k, 0, 0)),
            pl.BlockSpec((None, None, None, D, FFN_FC), lambda m, k: (layer, idx, nk + k, 0, 0)),
            pl.BlockSpec((None, None, FFN_FC, D), lambda m, k: (layer, idx, k, 0)),
        ],
        out_specs=pl.BlockSpec((FFN_TM, D), lambda m, k: (m, 0)),
        out_shape=jax.ShapeDtypeStruct((n_tok, D), f32),
        scratch_shapes=[pltpu.VMEM((2, FFN_TM, D), bf16), pltpu.VMEM((FFN_TM, D), f32)],
        compiler_params=_params(("arbitrary", "arbitrary"), 56),
        name="swiglu_half_step",
    )(x, x, modg, modg, modg, modg, modg, norm_g, w_up, w_up, w_down)


def _proj_kernel(x_ref, shift_ref, scale_ref, ng_ref, w_ref, o_ref, h_scr):
    @pl.when(pl.program_id(1) == 0)
    def _():
        h_scr[...] = _modulated_norm(x_ref[...], ng_ref[...], scale_ref[...], shift_ref[...]).astype(bf16)

    o_ref[...] = _dot(h_scr[...], w_ref[...])


def _proj_call(x, modg, norm_g, w_in, layer):
    n_tok = x.shape[0]
    gm = PROJ_TM // GROUP
    row = lambda r: pl.BlockSpec((None, gm, D), lambda m, n: (r, m, 0))
    return pl.pallas_call(
        _proj_kernel,
        grid=(n_tok // PROJ_TM, N_PROJ // PROJ_TN),
        in_specs=[
            pl.BlockSpec((PROJ_TM, D), lambda m, n: (m, 0)),
            row(3), row(4),
            pl.BlockSpec((None, None, 1, D), lambda m, n: (layer, 1, 0, 0)),
            pl.BlockSpec((None, None, D, PROJ_TN), lambda m, n: (layer, n, 0, 0)),
        ],
        out_specs=pl.BlockSpec((PROJ_TM, PROJ_TN), lambda m, n: (m, n)),
        out_shape=jax.ShapeDtypeStruct((n_tok, N_PROJ), f32),
        scratch_shapes=[pltpu.VMEM((PROJ_TM, D), bf16)],
        compiler_params=_params(("arbitrary", "arbitrary"), 40),
        name="mixer_in_proj",
    )(x, modg, modg, norm_g, w_in)


def _out_kernel(x_ref, gate_ref, a_ref, g_ref, w_ref, o_ref):
    y = _dot(a_ref[...], w_ref[:D_ATT, :]) + _dot(g_ref[...], w_ref[D_ATT:, :])
    o_ref[...] = x_ref[...] + _gated(y, gate_ref[...])


def _out_call(x, modg, o_att, o_gdn, w_out, layer):
    n_tok = x.shape[0]
    gm = OUT_TM // GROUP
    return pl.pallas_call(
        _out_kernel,
        grid=(n_tok // OUT_TM,),
        in_specs=[
            pl.BlockSpec((OUT_TM, D), lambda m: (m, 0)),
            pl.BlockSpec((None, gm, D), lambda m: (5, m, 0)),
            pl.BlockSpec((OUT_TM, D_ATT), lambda m: (m, 0)),
            pl.BlockSpec((OUT_TM, D_GDN), lambda m: (m, 0)),
            pl.BlockSpec((None, D, D), lambda m: (layer, 0, 0)),
        ],
        out_specs=pl.BlockSpec((OUT_TM, D), lambda m: (m, 0)),
        out_shape=jax.ShapeDtypeStruct((n_tok, D), f32),
        compiler_params=_params(("arbitrary",), 48),
        name="mixer_out_proj",
    )(x, modg, o_att, o_gdn, w_out)


def _bias_from_rel(rel, table_ref, h):
    n = jnp.abs(rel)
    big = jnp.full_like(n, 8)
    for t in (12, 16, 23, 32, 46, 64, FAR_DIST):
        big = big + (n >= t).astype(jnp.int32)
    bucket = jnp.where(n < 8, n, big) + jnp.where(rel > 0, NUM_BUCKETS // 2, 0)
    out = jnp.zeros(rel.shape, f32)
    for b in range(NUM_BUCKETS):
        out = jnp.where(bucket == b, table_ref[b, h], out)
    return out


def _fill_bias(dst_ref, lead, n_rows, n_cols, col0, key_axis, key_base, table_ref, h, scale=1.0):
    far = table_ref[FAR_BUCKET, h]
    for r0 in range(0, n_rows, SUB):
        nr = min(SUB, n_rows - r0)
        for c0 in range(0, n_cols, SUB):
            r = lax.broadcasted_iota(jnp.int32, (nr, SUB), 0) + r0
            c = lax.broadcasted_iota(jnp.int32, (nr, SUB), 1) + c0
            key, query = (r + key_base, c) if key_axis == 0 else (c + key_base, r)
            k_lo, q_lo = (r0 + key_base, c0) if key_axis == 0 else (c0 + key_base, r0)
            k_hi = k_lo + (nr if key_axis == 0 else SUB) - 1
            q_hi = q_lo + (SUB if key_axis == 0 else nr) - 1
            if k_lo // CHUNK > q_hi // CHUNK:
                tile = jnp.full((nr, SUB), MASKED, f32)
            elif k_hi - q_lo <= -FAR_DIST:
                tile = jnp.zeros((nr, SUB), f32)
            else:
                tile = (_bias_from_rel(key - query, table_ref, h) - far) * scale
                tile = jnp.where(key // CHUNK <= query // CHUNK, tile, MASKED)
            dst_ref[lead + (slice(r0, r0 + nr), slice(col0 + c0, col0 + c0 + SUB))] = tile


def _bias_kernel(table_ref, pb_ref, sb_ref):
    h = pl.program_id(0)
    _fill_bias(pb_ref, (0,), 2 * ATT_TQ, ATT_TQ, 0, 0, 0, table_ref, h, LOG2E)
    _fill_bias(pb_ref, (1,), 2 * ATT_TQ, ATT_TQ, 0, 0, -ATT_TQ, table_ref, h, LOG2E)
    _fill_bias(sb_ref, (), DEC_SEQ, SATT_TK, 0, 1, -SATT_TK, table_ref, h)
    _fill_bias(sb_ref, (), DEC_SEQ, HD, SATT_TK, 1, 0, table_ref, h)


def _bias_call(rpb_table):
    return pl.pallas_call(
        _bias_kernel,
        grid=(H,),
        in_specs=[pl.BlockSpec(memory_space=pltpu.SMEM)],
        out_specs=[
            pl.BlockSpec((None, 2, 2 * ATT_TQ, ATT_TQ), lambda h: (h, 0, 0, 0)),
            pl.BlockSpec((None, DEC_SEQ, SATT_TK + HD), lambda h: (h, 0, 0)),
        ],
        out_shape=[
            jax.ShapeDtypeStruct((H, 2, 2 * ATT_TQ, ATT_TQ), f32),
            jax.ShapeDtypeStruct((H, DEC_SEQ, SATT_TK + HD), f32),
        ],
        compiler_params=_params(("arbitrary",), 32),
        name="rel_pos_bias_tiles",
    )(rpb_table)


def _two_map_queries(qn):
    lane = lax.broadcasted_iota(jnp.int32, qn.shape, 1)
    return jnp.concatenate([jnp.where(lane < HALF, qn, 0.0), jnp.where(lane >= HALF, qn, 0.0)], axis=0).astype(bf16)


def _add_map_bias(s, b):
    n = b.shape[0]
    return (s.reshape(2, n, s.shape[1]) + b[None]).reshape(s.shape)


def _diff_combine(l, acc, lam_ref, sw, lam_init):
    n = acc.shape[0] // 2
    lq = lam_ref[...]
    lam = (jnp.exp(jnp.sum(lq[0:1] * lq[1:2], axis=-1, keepdims=True))
           - jnp.exp(jnp.sum(lq[2:3] * lq[3:4], axis=-1, keepdims=True)) + lam_init)
    o = acc[:n] / l[:n] - lam * (acc[n:] / l[n:])
    o = o * lax.rsqrt(jnp.mean(o * o, axis=-1, keepdims=True) + EPS) * sw
    return o * (1.0 - lam_init)


def _attn_prompt_kernel(lam_ref, qw_ref, kw_ref, sw_ref, bias_ref, q_ref, k_ref, v_ref,
                        o_ref, knew_ref, vnew_ref, kn_scr, vt_scr, *, lam_init):
    qi = pl.program_id(2)
    bd = _blockdiag_half()

    @pl.when(qi == 0)
    def _():
        kn = _half_rms(k_ref[...], kw_ref[...], bd)
        knew_ref[...] = kn
        kn_scr[...] = kn.astype(bf16)
        for c in range(SEQ // ATT_TK):
            v = v_ref[c * ATT_TK:(c + 1) * ATT_TK, :]
            vnew_ref[c * ATT_TK:(c + 1) * ATT_TK, :] = v
            vt_scr[c] = v.T.astype(bf16)

    qt = (_half_rms(q_ref[...], qw_ref[...], bd) * (HALF ** -0.5 * LOG2E)).T
    feat = lax.broadcasted_iota(jnp.int32, qt.shape, 0)
    q2t = jnp.concatenate([jnp.where(feat < HALF, qt, 0.0), jnp.where(feat >= HALF, qt, 0.0)], axis=1).astype(bf16)
    q_groups = [q2t[:, g * ATT_LG:(g + 1) * ATT_LG] for g in range(ATT_NG)]

    def logits(j, groups):
        keys = kn_scr[pl.ds(pl.multiple_of(j * ATT_TK, ATT_TK), ATT_TK), :]
        return [_dot(keys, q_groups[g]) for g in groups]

    def consume(j, g, s, state, bias):
        m, l, acc, p_prev = state
        acc = acc + _dot(vt_scr[jnp.maximum(j - 1, 0)], p_prev)
        if bias is not None:
            q0 = (g * ATT_LG) % ATT_TQ
            s = s + bias[:, q0:q0 + ATT_LG]
        m_new = jnp.maximum(m, jnp.max(s, axis=0, keepdims=True))
        p = jnp.exp2(s - m_new)
        alpha = jnp.exp2(m - m_new)
        return m_new, alpha * l + jnp.sum(p, axis=0, keepdims=True), alpha * acc, p.astype(bf16)

    def step(j, states, bias=None):
        s = logits(j, range(ATT_NG))
        return [consume(j, g, s[g], states[g], bias) for g in range(ATT_NG)]

    states = [(jnp.full((1, ATT_LG), MASKED, f32), jnp.zeros((1, ATT_LG), f32), jnp.zeros((HD, ATT_LG), f32),
               jnp.zeros((ATT_TK, ATT_LG), bf16)) for _ in range(ATT_NG)]
    n_near = 2 * ATT_TQ // ATT_TK
    near0 = jnp.maximum(qi - 1, 0) * (ATT_TQ // ATT_TK)
    states = lax.fori_loop(0, near0, step, states)
    variant = jnp.minimum(qi, 1)
    for part in range(n_near):
        states = step(near0 + part, states, bias_ref[variant, part * ATT_TK:(part + 1) * ATT_TK, :])
    states = [(m, l, acc + _dot(vt_scr[near0 + n_near - 1], p)) for m, l, acc, p in states]
    l = jnp.concatenate([st[1] for st in states], axis=1)
    acc = jnp.concatenate([st[2] for st in states], axis=1)

    lq = lam_ref[...]
    lam = (jnp.exp(jnp.sum(lq[0:1] * lq[1:2], axis=-1, keepdims=True))
           - jnp.exp(jnp.sum(lq[2:3] * lq[3:4], axis=-1, keepdims=True)) + lam_init)
    o2 = acc / l
    o = (o2[:, :ATT_TQ] - lam * o2[:, ATT_TQ:]).T
    o = o * lax.rsqrt(jnp.mean(o * o, axis=-1, keepdims=True) + EPS) * sw_ref[...]
    o_ref[...] = (o * (1.0 - lam_init)).astype(bf16)


def _attn_prompt_call(p, bias, lam_qk, qk_norm2, subln_w, layer, lam_init):
    nq = SEQ // ATT_TQ
    vec = lambda r: pl.BlockSpec((None, None, 1, HD), lambda b, h, i: (layer, r, 0, 0))
    head_seq = pl.BlockSpec((SEQ, HD), lambda b, h, i: (b, h))
    return pl.pallas_call(
        functools.partial(_attn_prompt_kernel, lam_init=lam_init),
        grid=(BATCH, H, nq),
        in_specs=[
            pl.BlockSpec((None, 4, HALF), lambda b, h, i: (layer, 0, 0)),
            vec(0), vec(1),
            pl.BlockSpec((None, 1, HD), lambda b, h, i: (layer, 0, 0)),
            pl.BlockSpec((None, 2, 2 * ATT_TQ, ATT_TQ), lambda b, h, i: (h, 0, 0, 0)),
            pl.BlockSpec((ATT_TQ, HD), lambda b, h, i: (b * nq + i, h)),
            pl.BlockSpec((SEQ, HD), lambda b, h, i: (b, H + h)),
            pl.BlockSpec((SEQ, HD), lambda b, h, i: (b, 2 * H + h)),
        ],
        out_specs=[pl.BlockSpec((ATT_TQ, HD), lambda b, h, i: (b * nq + i, h)), head_seq, head_seq],
        out_shape=[
            jax.ShapeDtypeStruct((P_TOK, D_ATT), bf16),
            jax.ShapeDtypeStruct((P_TOK, D_ATT), f32),
            jax.ShapeDtypeStruct((P_TOK, D_ATT), f32),
        ],
        scratch_shapes=[pltpu.VMEM((SEQ, HD), bf16), pltpu.VMEM((SEQ // ATT_TK, HD, ATT_TK), bf16)],
        compiler_params=_params(("arbitrary", "arbitrary", "arbitrary"), 56),
        name="diff_attn_prompt",
    )(lam_qk, qk_norm2, qk_norm2, subln_w.reshape(DEPTH, 1, HD), bias, p, p, p)


def _attn_sample_kernel(lam_ref, qw_ref, kw_ref, sw_ref, bias_ref, q_ref, kx_ref, vx_ref, ck_ref, cv_ref,
                        o_ref, knew_ref, vnew_ref, *, lam_init):
    bd = _blockdiag_half()
    n_blk = PAST // SATT_TK
    heads, blocks = range(H), range(n_blk)
    cols = [slice(h * HD, (h + 1) * HD) for h in heads]
    vnew_ref[...] = vx_ref[...]
    q2 = [_two_map_queries(_half_rms(q_ref[:, cols[h]], qw_ref[...], bd) * (HALF ** -0.5)) for h in heads]
    kn = [_half_rms(kx_ref[:, cols[h]], kw_ref[...], bd) for h in heads]
    for h in heads:
        knew_ref[:, cols[h]] = kn[h]

    def cached(ref, h, i):
        return ref[pl.ds(i * SATT_TK * H + h, SATT_TK, stride=H), :].astype(bf16)

    ss = [[_dot_nt(q2[h], cached(ck_ref, h, i)) for i in blocks] for h in heads]
    for h in heads:
        ss[h][-1] = _add_map_bias(ss[h][-1], bias_ref[h, :, :SATT_TK])
        ss[h].append(_add_map_bias(_dot_nt(q2[h], kn[h].astype(bf16)), bias_ref[h, :, SATT_TK:SATT_TK + DEC_SEQ]))
    for h in heads:
        m = functools.reduce(jnp.maximum, [jnp.max(s, axis=-1, keepdims=True) for s in ss[h]])
        ps = [jnp.exp(s - m) for s in ss[h]]
        l = functools.reduce(jnp.add, [jnp.sum(p, axis=-1, keepdims=True) for p in ps])
        vals = [cached(cv_ref, h, i) for i in blocks] + [vx_ref[:, cols[h]].astype(bf16)]
        acc = functools.reduce(jnp.add, [_dot(p.astype(bf16), v) for p, v in zip(ps, vals)])
        o_ref[:, cols[h]] = _diff_combine(l, acc, lam_ref, sw_ref[...], lam_init).astype(bf16)


def _attn_sample_call(p, bias, cache_k, cache_v, lam_qk, qk_norm2, subln_w, layer, lam_init):
    vec = lambda r: pl.BlockSpec((None, None, 1, HD), lambda b: (layer, r, 0, 0))
    sec = lambda c: pl.BlockSpec((DEC_SEQ, D_ATT), lambda b: (b, c))
    cache = pl.BlockSpec((None, None, PAST * H, HD), lambda b: (layer, b, 0, 0))
    stream = pl.BlockSpec((DEC_SEQ, D_ATT), lambda b: (b, 0))
    return pl.pallas_call(
        functools.partial(_attn_sample_kernel, lam_init=lam_init),
        grid=(DEC_BATCH,),
        in_specs=[
            pl.BlockSpec((None, 4, HALF), lambda b: (layer, 0, 0)),
            vec(0), vec(1),
            pl.BlockSpec((None, 1, HD), lambda b: (layer, 0, 0)),
            pl.BlockSpec((H, DEC_SEQ, SATT_TK + HD), lambda b: (0, 0, 0)),
            sec(0), sec(1), sec(2), cache, cache,
        ],
        out_specs=[stream, stream, stream],
        out_shape=[
            jax.ShapeDtypeStruct((S_TOK, D_ATT), bf16),
            jax.ShapeDtypeStruct((S_TOK, D_ATT), f32),
            jax.ShapeDtypeStruct((S_TOK, D_ATT), f32),
        ],
        compiler_params=_params(("arbitrary",), 60),
        name="diff_attn_sample",
    )(lam_qk, qk_norm2, qk_norm2, subln_w.reshape(DEPTH, 1, HD), bias, p, p, p, cache_k, cache_v)


def _gdn_kernel(p_ref, z_ref, ab_ref, hist_ref, s0_ref, cw_ref, alog_ref, dtb_ref, gw_ref,
                o_ref, sfin_ref, cnew_ref, xp_scr, s_scr, *, n_seq):
    per_seq = GDN_SLOTS // n_seq
    rows_seq = per_seq * CHUNK

    @pl.when(pl.program_id(1) == 0)
    def _():
        xp_scr[:, 0:8, :] = hist_ref[...]
        s_scr[...] = s0_ref[...]

    ys = []
    for sq in range(n_seq):
        xp_scr[sq, 8:8 + rows_seq, :] = p_ref[sq * rows_seq:(sq + 1) * rows_seq, :]
        y = xp_scr[sq, 5:5 + rows_seq, :] * cw_ref[0:1, :]
        for i in range(1, CONV_W):
            y = y + xp_scr[sq, 5 + i:5 + i + rows_seq, :] * cw_ref[i:i + 1, :]
        y = _silu(y)
        tail = xp_scr[sq, rows_seq:rows_seq + 8, :]
        xp_scr[sq, 0:8, :] = tail
        cnew_ref[sq] = tail
        ys += [y[c * CHUNK:(c + 1) * CHUNK] for c in range(per_seq)]

    ii = lax.broadcasted_iota(jnp.int32, (CHUNK, CHUNK), 0)
    jj = lax.broadcasted_iota(jnp.int32, (CHUNK, CHUNK), 1)
    causal = ii >= jj
    strict = ii > jj
    eye = jnp.where(ii == jj, 1.0, 0.0)

    units = [(n, h) for n in range(GDN_SLOTS) for h in range(H)]
    q16, k16, kb16, rhs16, gamma, q_dec, k_dec, decay = {}, {}, {}, {}, {}, {}, {}, {}
    for n in range(GDN_SLOTS):
        y = ys[n]
        ab = ab_ref[n * CHUNK:(n + 1) * CHUNK, :]
        pre = ab + dtb_ref[...]
        softplus = jnp.maximum(pre, 0.0) + jnp.log1p(jnp.exp(-jnp.abs(pre)))
        g_all = -jnp.exp(alog_ref[...]) * softplus
        beta_all = jax.nn.sigmoid(ab)
        for h in range(H):
            u = (n, h)
            qx = y[:, h * HD:(h + 1) * HD]
            kx = y[:, D_GDN + h * HD:D_GDN + (h + 1) * HD]
            v = y[:, 2 * D_GDN + h * HD:2 * D_GDN + (h + 1) * HD]
            q = qx * lax.rsqrt(jnp.sum(qx * qx, axis=-1, keepdims=True) + EPS) * (HD ** -0.5)
            k = kx * lax.rsqrt(jnp.sum(kx * kx, axis=-1, keepdims=True) + EPS)
            g = g_all[:, h:h + 1]
            beta = beta_all[:, H + h:H + h + 1]
            gc_row = jnp.sum(jnp.where(ii <= jj, g, 0.0), axis=0, keepdims=True)
            gc = jnp.sum(jnp.where(ii == jj, gc_row, 0.0), axis=1, keepdims=True)
            gc_last = jnp.sum(g, axis=0, keepdims=True)
            gamma[u] = jnp.where(causal, jnp.exp(jnp.where(causal, gc - gc_row, 0.0)), 0.0)
            egc = jnp.exp(gc)
            kb = k * beta
            q16[u], k16[u], kb16[u] = q.astype(bf16), k.astype(bf16), kb.astype(bf16)
            rhs16[u] = jnp.concatenate([kb * egc, v * beta], axis=-1).astype(bf16)
            q_dec[u] = q * egc
            k_dec[u] = k * jnp.exp(gc_last - gc)
            decay[u] = jnp.exp(gc_last)

    kk = {u: _dot_nt(kb16[u], k16[u]) for u in units}
    qk = {u: _dot_nt(q16[u], k16[u]) for u in units}
    a = {u: jnp.where(strict, kk[u] * gamma[u], 0.0) for u in units}
    attn = {u: qk[u] * gamma[u] for u in units}

    base = 8
    d = {u: jnp.where(ii // base == jj // base, a[u], 0.0) for u in units}
    t = {u: eye - d[u] for u in units}
    d16 = {u: d[u].astype(bf16) for u in units}
    pw16 = {u: _dot(d16[u], d16[u]).astype(bf16) for u in units}
    t = {u: t[u] + _dot(t[u].astype(bf16), pw16[u]) for u in units}
    pw16 = {u: _dot(pw16[u], pw16[u]).astype(bf16) for u in units}
    t = {u: t[u] + _dot(t[u].astype(bf16), pw16[u]) for u in units}
    size = base
    while size < CHUNK:
        off = (ii // (2 * size) == jj // (2 * size)) & (ii // size != jj // size)
        t16 = {u: t[u].astype(bf16) for u in units}
        ta = {u: _dot(t16[u], jnp.where(off, a[u], 0.0).astype(bf16)) for u in units}
        t = {u: t[u] - _dot(ta[u].astype(bf16), t16[u]) for u in units}
        size *= 2
    sol = {u: _dot(t[u].astype(bf16), rhs16[u]) for u in units}

    s_cur = {}
    for n in range(GDN_SLOTS):
        sq = n // per_seq
        if n % per_seq == 0:
            s_cur = {h: s_scr[sq, h] for h in range(H)}
        ws = {h: _dot(jnp.concatenate([sol[(n, h)][:, :HD], q_dec[(n, h)]], axis=0).astype(bf16), s_cur[h].astype(bf16))
              for h in range(H)}
        vn16 = {h: (sol[(n, h)][:, HD:] - ws[h][:CHUNK]).astype(bf16) for h in range(H)}
        r = {h: _dot(jnp.concatenate([attn[(n, h)], k_dec[(n, h)].T], axis=0).astype(bf16), vn16[h]) for h in range(H)}
        for h in range(H):
            cols = slice(h * HD, (h + 1) * HD)
            rows = slice(n * CHUNK, (n + 1) * CHUNK)
            s_cur[h] = s_cur[h] * decay[(n, h)] + r[h][CHUNK:]
            o = ws[h][CHUNK:] + r[h][:CHUNK]
            o = o * lax.rsqrt(jnp.mean(o * o, axis=-1, keepdims=True) + EPS) * gw_ref[...]
            o_ref[rows, cols] = (o * _silu(z_ref[rows, cols])).astype(bf16)
        if n % per_seq == per_seq - 1:
            for h in range(H):
                s_scr[sq, h] = s_cur[h]

    @pl.when(pl.program_id(1) == pl.num_programs(1) - 1)
    def _():
        sfin_ref[...] = s_scr[...]


def _gdn_call(p, hist, s0, conv_w, a_log, dt_bias, gdn_norm_w, layer, n_blocks, steps, n_seq):
    br = GDN_SLOTS * CHUNK
    rows = lambda col, width: pl.BlockSpec((br, width), lambda b, c: (b * steps + c, col))
    vec = pl.BlockSpec((None, 1, HD), lambda b, c: (layer, 0, 0))
    return pl.pallas_call(
        functools.partial(_gdn_kernel, n_seq=n_seq),
        grid=(n_blocks, steps),
        in_specs=[
            rows(COL_CONV // CONV_CH, CONV_CH),
            rows(COL_Z // D_GDN, D_GDN),
            rows(COL_AB // HD, HD),
            pl.BlockSpec((n_seq, 8, CONV_CH), lambda b, c: (b, 0, 0)),
            pl.BlockSpec((n_seq, H, HD, HD), lambda b, c: (b, 0, 0, 0)),
            pl.BlockSpec((None, CONV_W, CONV_CH), lambda b, c: (layer, 0, 0)),
            vec, vec, vec,
        ],
        out_specs=[
            pl.BlockSpec((br, D_GDN), lambda b, c: (b * steps + c, 0)),
            pl.BlockSpec((n_seq, H, HD, HD), lambda b, c: (b, 0, 0, 0)),
            pl.BlockSpec((n_seq, 8, CONV_CH), lambda b, c: (b, 0, 0)),
        ],
        out_shape=[
            jax.ShapeDtypeStruct((n_blocks * steps * br, D_GDN), bf16),
            jax.ShapeDtypeStruct((n_blocks * n_seq, H, HD, HD), f32),
            jax.ShapeDtypeStruct((n_blocks * n_seq, 8, CONV_CH), f32),
        ],
        scratch_shapes=[pltpu.VMEM((n_seq, 8 + br // n_seq, CONV_CH), f32), pltpu.VMEM((n_seq, H, HD, HD), f32)],
        compiler_params=_params(("arbitrary", "arbitrary"), 48),
        name="gated_deltanet",
    )(p, p, p, hist, s0, conv_w, a_log, dt_bias, gdn_norm_w)


def _pad_lanes(v):
    return jnp.pad(v, ((0, 0), (0, HD - v.shape[-1]))).reshape(DEPTH, 1, HD)


def kernel(x_prompt, x_sample, cache_k, cache_v, state_gdn, state_conv, c_prompt, c_sample, w_mod, b_mod, norm_g, ffn_up, ffn_down, w_in, w_out, qk_norm, lam_qk, subln_w, rpb_table, conv_w, a_log, dt_bias, gdn_norm_w):
    xp = x_prompt.reshape(P_TOK, D)
    xs = x_sample.reshape(S_TOK, D)
    c_all = jnp.concatenate([c_prompt, c_sample, jnp.zeros((N_CBATCH - BATCH - DEC_BATCH, D), f32)], axis=0)

    w_up16 = jnp.transpose(ffn_up.astype(bf16).reshape(DEPTH, 2, D, 2 * D_FF // FFN_FC, FFN_FC), (0, 1, 3, 2, 4))
    w_down16 = ffn_down.astype(bf16)
    w_in16 = jnp.pad(w_in.astype(bf16), ((0, 0), (0, 0), (0, N_PROJ - N_IN)))
    w_in16 = jnp.transpose(w_in16.reshape(DEPTH, D, N_PROJ // PROJ_TN, PROJ_TN), (0, 2, 1, 3))
    w_out16 = w_out.astype(bf16)
    norm_g4 = norm_g.reshape(DEPTH, 3, 1, D)
    qk_norm2 = jnp.concatenate([qk_norm, qk_norm], axis=-1).reshape(DEPTH, 2, 1, HD)
    gdn_w = gdn_norm_w.reshape(DEPTH, 1, HD)
    a_log_p, dt_bias_p = _pad_lanes(a_log), _pad_lanes(dt_bias)
    ck = cache_k.reshape(DEPTH, DEC_BATCH, PAST * H, HD)
    cv = cache_v.reshape(DEPTH, DEC_BATCH, PAST * H, HD)
    hist_s = jnp.pad(state_conv, ((0, 0), (0, 0), (8 - (CONV_W - 1), 0), (0, 0)))
    hist_p = jnp.zeros((BATCH, 8, CONV_CH), f32)
    s0_p = jnp.zeros((BATCH, H, HD, HD), f32)

    mod = _mod_call(c_all, w_mod, b_mod).reshape(DEPTH, N_CBATCH, N_MOD, D)
    modg_p = jnp.transpose(jnp.repeat(mod[:, :BATCH], SEQ // GROUP, axis=1), (0, 2, 1, 3))
    modg_s = jnp.transpose(mod[:, BATCH:BATCH + DEC_BATCH], (0, 2, 1, 3))

    pbias, sbias = _bias_call(rpb_table)

    gdn_rows = GDN_SLOTS * CHUNK
    outs = {n: [] for n in ("kp", "vp", "sp", "cp", "ks", "vs", "ss", "cs")}
    for l in range(DEPTH):
        lam_init = 0.8 - 0.6 * math.exp(-0.3 * l)
        xp = _ffn_call(xp, modg_p[l], norm_g4, w_up16, w_down16, l, 0, 0)
        xs = _ffn_call(xs, modg_s[l], norm_g4, w_up16, w_down16, l, 0, 0)
        pp = _proj_call(xp, modg_p[l], norm_g4, w_in16, l)
        ps = _proj_call(xs, modg_s[l], norm_g4, w_in16, l)

        oa_p, k_p, v_p = _attn_prompt_call(pp, pbias, lam_qk, qk_norm2, subln_w, l, lam_init)
        oa_s, k_s, v_s = _attn_sample_call(ps, sbias, ck, cv, lam_qk, qk_norm2, subln_w, l, lam_init)
        og_p, s_p, c_p = _gdn_call(pp, hist_p, s0_p, conv_w, a_log_p, dt_bias_p, gdn_w, l, BATCH, SEQ // gdn_rows, 1)
        og_s, s_s, c_s = _gdn_call(ps, hist_s[l], state_gdn[l], conv_w, a_log_p, dt_bias_p, gdn_w, l,
                                   DEC_BATCH // GDN_SLOTS, 1, GDN_SLOTS)

        xp = _out_call(xp, modg_p[l], oa_p, og_p, w_out16, l)
        xs = _out_call(xs, modg_s[l], oa_s, og_s, w_out16, l)
        xp = _ffn_call(xp, modg_p[l], norm_g4, w_up16, w_down16, l, 1, 2)
        xs = _ffn_call(xs, modg_s[l], norm_g4, w_up16, w_down16, l, 1, 2)

        outs["kp"].append(k_p.reshape(BATCH, SEQ, H, HD))
        outs["vp"].append(v_p.reshape(BATCH, SEQ, H, HD))
        outs["sp"].append(s_p)
        outs["cp"].append(c_p[:, 8 - (CONV_W - 1):])
        outs["ks"].append(k_s.reshape(DEC_BATCH, DEC_SEQ, H, HD))
        outs["vs"].append(v_s.reshape(DEC_BATCH, DEC_SEQ, H, HD))
        outs["ss"].append(s_s)
        outs["cs"].append(c_s[:, 8 - (CONV_W - 1):])

    st = {n: jnp.stack(v) for n, v in outs.items()}
    return (xp.reshape(BATCH, SEQ, D), xs.reshape(DEC_BATCH, DEC_SEQ, D),
            st["kp"], st["vp"], st["sp"], st["cp"], st["ks"], st["vs"], st["ss"], st["cs"])
```

```python
import functools
import math

import jax
import jax.numpy as jnp
from jax import lax
from jax.experimental import pallas as pl
from jax.experimental.pallas import tpu as pltpu

f32 = jnp.float32
bf16 = jnp.bfloat16

D = 2048
BATCH = 2
SEQ = 4096
DEPTH = 2
DEC_BATCH = 32
DEC_SEQ = 64
PAST = 2048
CHUNK = 64
H = 8
HD = 128
HALF = 64
D_ATT = H * HD
D_GDN = H * HD
CONV_W = 4
CONV_CH = 3 * D_GDN
D_FF = 5632
NUM_BUCKETS = 32
FAR_BUCKET = 15
FAR_DIST = 91
N_MOD = 9
N_IN = 3 * D_ATT + CONV_CH + D_GDN + 2 * H
EPS = 1e-6

GROUP = 64
P_TOK = BATCH * SEQ
S_TOK = DEC_BATCH * DEC_SEQ
N_CBATCH = 40
N_PROJ = 7680
COL_K, COL_V, COL_CONV, COL_Z, COL_AB = D_ATT, 2 * D_ATT, 3 * D_ATT, 3 * D_ATT + CONV_CH, 3 * D_ATT + CONV_CH + D_GDN

MASKED = -1e30
SUB = 128

FFN_TM, FFN_FC = 512, 512
PROJ_TM, PROJ_TN = 1024, 512
OUT_TM = 512
MOD_TN = 1024
ATT_TQ = 512
ATT_TK = 512
ATT_NG = 4
ATT_LG = 2 * ATT_TQ // ATT_NG
LOG2E = math.log2(math.e)
MAX_SPAN = 100.0
SATT_TK = 512
GDN_SLOTS = 2


def _params(sem, vmem_mb):
    return pltpu.CompilerParams(dimension_semantics=sem, vmem_limit_bytes=vmem_mb * 1024 * 1024)


def _dot(a, b):
    return jnp.dot(a, b, preferred_element_type=f32)


def _dot_nt(a, b):
    return lax.dot_general(a, b, (((1,), (1,)), ((), ())), preferred_element_type=f32)


def _silu(x):
    return x * jax.nn.sigmoid(x)


def _modulated_norm(x, ng, scale, shift):
    tm = x.shape[0]
    ms = jnp.mean(x * x, axis=-1, keepdims=True)
    y = x * lax.rsqrt(ms + EPS) * ng
    y = y.reshape(tm // GROUP, GROUP, D) * (1.0 + scale[:, None, :]) + shift[:, None, :]
    return y.reshape(tm, D)


def _gated(y, gate):
    tm = y.shape[0]
    return (y.reshape(tm // GROUP, GROUP, D) * gate[:, None, :]).reshape(tm, D)


def _blockdiag_half():
    r = lax.broadcasted_iota(jnp.int32, (HD, HD), 0) // HALF
    c = lax.broadcasted_iota(jnp.int32, (HD, HD), 1) // HALF
    return (r == c).astype(bf16)


def _half_rms(x, w, bd):
    ss = _dot((x * x).astype(bf16), bd) * (1.0 / HALF)
    return x * lax.rsqrt(ss + EPS) * w


def _mod_kernel(c_ref, w_ref, b_ref, o_ref):
    a = _silu(c_ref[...]).astype(bf16)
    o_ref[...] = _dot(a, w_ref[...].astype(bf16)) + b_ref[...]


def _mod_call(c_all, w_mod, b_mod):
    n = N_MOD * D // MOD_TN
    return pl.pallas_call(
        _mod_kernel,
        grid=(DEPTH, n),
        in_specs=[
            pl.BlockSpec((N_CBATCH, D), lambda l, j: (0, 0)),
            pl.BlockSpec((None, D, MOD_TN), lambda l, j: (l, 0, j)),
            pl.BlockSpec((None, 1, MOD_TN), lambda l, j: (l, 0, j)),
        ],
        out_specs=pl.BlockSpec((None, N_CBATCH, MOD_TN), lambda l, j: (l, 0, j)),
        out_shape=jax.ShapeDtypeStruct((DEPTH, N_CBATCH, N_MOD * D), f32),
        compiler_params=_params(("arbitrary", "arbitrary"), 40),
        name="adaln_mod",
    )(c_all, w_mod, b_mod.reshape(DEPTH, 1, N_MOD * D))


def _ffn_kernel(x_ref, xn_ref, shift0_ref, scale0_ref, shiftn_ref, scalen_ref, gate_ref, ng_ref,
                wg_ref, wu_ref, wd_ref, o_ref, h_scr, acc_scr):
    m, k = pl.program_id(0), pl.program_id(1)
    slot = m % 2

    @pl.when((m == 0) & (k == 0))
    def _():
        h_scr[0] = _modulated_norm(x_ref[...], ng_ref[...], scale0_ref[...], shift0_ref[...]).astype(bf16)

    def ff_chunk():
        h = h_scr[slot]
        act = (_silu(_dot(h, wg_ref[...])) * _dot(h, wu_ref[...])).astype(bf16)
        acc_scr[...] = jnp.where(k == 0, 0.0, acc_scr[...]) + _dot(act, wd_ref[...])

    @pl.when(k < pl.num_programs(1) - 1)
    def _():
        ff_chunk()

    @pl.when(k == pl.num_programs(1) - 1)
    def _():
        ff_chunk()
        h_scr[1 - slot] = _modulated_norm(xn_ref[...], ng_ref[...], scalen_ref[...], shiftn_ref[...]).astype(bf16)
        o_ref[...] = x_ref[...] + 0.5 * _gated(acc_scr[...], gate_ref[...])


def _ffn_call(x, modg, norm_g, w_up, w_down, layer, idx, j):
    n_tok = x.shape[0]
    nm = n_tok // FFN_TM
    nk = D_FF // FFN_FC
    gm = FFN_TM // GROUP
    nxt = lambda m: jnp.minimum(m + 1, nm - 1)
    return pl.pallas_call(
        _ffn_kernel,
        grid=(nm, nk),
        in_specs=[
            pl.BlockSpec((FFN_TM, D), lambda m, k: (m, 0)),
            pl.BlockSpec((FFN_TM, D), lambda m, k: (nxt(m), 0)),
            pl.BlockSpec((None, gm, D), lambda m, k: (3 * j, 0, 0)),
            pl.BlockSpec((None, gm, D), lambda m, k: (3 * j + 1, 0, 0)),
            pl.BlockSpec((None, gm, D), lambda m, k: (3 * j, nxt(m), 0)),
            pl.BlockSpec((None, gm, D), lambda m, k: (3 * j + 1, nxt(m), 0)),
            pl.BlockSpec((None, gm, D), lambda m, k: (3 * j + 2, m, 0)),
            pl.BlockSpec((None, None, 1, D), lambda m, k: (layer, j, 0, 0)),
            pl.BlockSpec((None, None, D, FFN_FC), lambda m, k: (layer, idx, 0, k)),
            pl.BlockSpec((None, None, D, FFN_FC), lambda m, k: (layer, idx, 0, nk + k)),
            pl.BlockSpec((None, None, FFN_FC, D), lambda m, k: (layer, idx, k, 0)),
        ],
        out_specs=pl.BlockSpec((FFN_TM, D), lambda m, k: (m, 0)),
        out_shape=jax.ShapeDtypeStruct((n_tok, D), f32),
        scratch_shapes=[pltpu.VMEM((2, FFN_TM, D), bf16), pltpu.VMEM((FFN_TM, D), f32)],
        compiler_params=_params(("arbitrary", "arbitrary"), 56),
        name="swiglu_half_step",
    )(x, x, modg, modg, modg, modg, modg, norm_g, w_up, w_up, w_down)


def _proj_kernel(x_ref, shift_ref, scale_ref, ng_ref, w_ref, o_ref, h_scr):
    @pl.when(pl.program_id(1) == 0)
    def _():
        h_scr[...] = _modulated_norm(x_ref[...], ng_ref[...], scale_ref[...], shift_ref[...]).astype(bf16)

    o_ref[...] = _dot(h_scr[...], w_ref[...])


def _proj_call(x, modg, norm_g, w_in, layer):
    n_tok = x.shape[0]
    gm = PROJ_TM // GROUP
    row = lambda r: pl.BlockSpec((None, gm, D), lambda m, n: (r, m, 0))
    return pl.pallas_call(
        _proj_kernel,
        grid=(n_tok // PROJ_TM, N_PROJ // PROJ_TN),
        in_specs=[
            pl.BlockSpec((PROJ_TM, D), lambda m, n: (m, 0)),
            row(3), row(4),
            pl.BlockSpec((None, None, 1, D), lambda m, n: (layer, 1, 0, 0)),
            pl.BlockSpec((None, D, PROJ_TN), lambda m, n: (layer, 0, n)),
        ],
        out_specs=pl.BlockSpec((PROJ_TM, PROJ_TN), lambda m, n: (m, n)),
        out_shape=jax.ShapeDtypeStruct((n_tok, N_PROJ), f32),
        scratch_shapes=[pltpu.VMEM((PROJ_TM, D), bf16)],
        compiler_params=_params(("arbitrary", "arbitrary"), 40),
        name="mixer_in_proj",
    )(x, modg, modg, norm_g, w_in)


def _out_kernel(x_ref, gate_ref, a_ref, g_ref, w_ref, o_ref):
    y = _dot(a_ref[...], w_ref[:D_ATT, :]) + _dot(g_ref[...], w_ref[D_ATT:, :])
    o_ref[...] = x_ref[...] + _gated(y, gate_ref[...])


def _out_call(x, modg, o_att, o_gdn, w_out, layer):
    n_tok = x.shape[0]
    gm = OUT_TM // GROUP
    return pl.pallas_call(
        _out_kernel,
        grid=(n_tok // OUT_TM,),
        in_specs=[
            pl.BlockSpec((OUT_TM, D), lambda m: (m, 0)),
            pl.BlockSpec((None, gm, D), lambda m: (5, m, 0)),
            pl.BlockSpec((OUT_TM, D_ATT), lambda m: (m, 0)),
            pl.BlockSpec((OUT_TM, D_GDN), lambda m: (m, 0)),
            pl.BlockSpec((None, D, D), lambda m: (layer, 0, 0)),
        ],
        out_specs=pl.BlockSpec((OUT_TM, D), lambda m: (m, 0)),
        out_shape=jax.ShapeDtypeStruct((n_tok, D), f32),
        compiler_params=_params(("arbitrary",), 48),
        name="mixer_out_proj",
    )(x, modg, o_att, o_gdn, w_out)


def _bias_from_rel(rel, table_ref, h):
    n = jnp.abs(rel)
    big = jnp.full_like(n, 8)
    for t in (12, 16, 23, 32, 46, 64, FAR_DIST):
        big = big + (n >= t).astype(jnp.int32)
    bucket = jnp.where(n < 8, n, big) + jnp.where(rel > 0, NUM_BUCKETS // 2, 0)
    out = jnp.zeros(rel.shape, f32)
    for b in range(NUM_BUCKETS):
        out = jnp.where(bucket == b, table_ref[b, h], out)
    return out


def _fill_bias(dst_ref, lead, n_rows, n_cols, col0, key_axis, key_base, table_ref, h, scale=1.0):
    far = table_ref[FAR_BUCKET, h]
    for r0 in range(0, n_rows, SUB):
        nr = min(SUB, n_rows - r0)
        for c0 in range(0, n_cols, SUB):
            r = lax.broadcasted_iota(jnp.int32, (nr, SUB), 0) + r0
            c = lax.broadcasted_iota(jnp.int32, (nr, SUB), 1) + c0
            key, query = (r + key_base, c) if key_axis == 0 else (c + key_base, r)
            k_lo, q_lo = (r0 + key_base, c0) if key_axis == 0 else (c0 + key_base, r0)
            k_hi = k_lo + (nr if key_axis == 0 else SUB) - 1
            q_hi = q_lo + (SUB if key_axis == 0 else nr) - 1
            if k_lo // CHUNK > q_hi // CHUNK:
                tile = jnp.full((nr, SUB), MASKED, f32)
            elif k_hi - q_lo <= -FAR_DIST:
                tile = jnp.zeros((nr, SUB), f32)
            else:
                tile = (_bias_from_rel(key - query, table_ref, h) - far) * scale
                tile = jnp.where(key // CHUNK <= query // CHUNK, tile, MASKED)
            dst_ref[lead + (slice(r0, r0 + nr), slice(col0 + c0, col0 + c0 + SUB))] = tile


def _bias_kernel(table_ref, pb_ref, sb_ref):
    h = pl.program_id(0)
    _fill_bias(pb_ref, (0,), 2 * ATT_TQ, ATT_TQ, 0, 0, 0, table_ref, h, LOG2E)
    _fill_bias(pb_ref, (1,), 2 * ATT_TQ, ATT_TQ, 0, 0, -ATT_TQ, table_ref, h, LOG2E)
    _fill_bias(sb_ref, (), DEC_SEQ, SATT_TK, 0, 1, -SATT_TK, table_ref, h)
    _fill_bias(sb_ref, (), DEC_SEQ, HD, SATT_TK, 1, 0, table_ref, h)


def _bias_call(rpb_table):
    return pl.pallas_call(
        _bias_kernel,
        grid=(H,),
        in_specs=[pl.BlockSpec(memory_space=pltpu.SMEM)],
        out_specs=[
            pl.BlockSpec((None, 2, 2 * ATT_TQ, ATT_TQ), lambda h: (h, 0, 0, 0)),
            pl.BlockSpec((None, DEC_SEQ, SATT_TK + HD), lambda h: (h, 0, 0)),
        ],
        out_shape=[
            jax.ShapeDtypeStruct((H, 2, 2 * ATT_TQ, ATT_TQ), f32),
            jax.ShapeDtypeStruct((H, DEC_SEQ, SATT_TK + HD), f32),
        ],
        compiler_params=_params(("arbitrary",), 32),
        name="rel_pos_bias_tiles",
    )(rpb_table)


def _two_map_queries(qn):
    lane = lax.broadcasted_iota(jnp.int32, qn.shape, 1)
    return jnp.concatenate([jnp.where(lane < HALF, qn, 0.0), jnp.where(lane >= HALF, qn, 0.0)], axis=0).astype(bf16)


def _add_map_bias(s, b):
    n = b.shape[0]
    return (s.reshape(2, n, s.shape[1]) + b[None]).reshape(s.shape)


def _diff_combine(l, acc, lam_ref, sw, lam_init):
    n = acc.shape[0] // 2
    lq = lam_ref[...]
    lam = (jnp.exp(jnp.sum(lq[0:1] * lq[1:2], axis=-1, keepdims=True))
           - jnp.exp(jnp.sum(lq[2:3] * lq[3:4], axis=-1, keepdims=True)) + lam_init)
    o = acc[:n] / l[:n] - lam * (acc[n:] / l[n:])
    o = o * lax.rsqrt(jnp.mean(o * o, axis=-1, keepdims=True) + EPS) * sw
    return o * (1.0 - lam_init)


def _attn_prompt_kernel(fast_ref, lam_ref, qw_ref, kw_ref, sw_ref, bias_ref, q_ref, k_ref, v_ref,
                        o_ref, knew_ref, vnew_ref, kn_scr, vt_scr, *, layer, lam_init):
    qi = pl.program_id(2)
    bd = _blockdiag_half()

    @pl.when(qi == 0)
    def _():
        kn = _half_rms(k_ref[...], kw_ref[...], bd)
        knew_ref[...] = kn
        kn_scr[...] = kn.astype(bf16)
        for c in range(SEQ // ATT_TK):
            v = v_ref[c * ATT_TK:(c + 1) * ATT_TK, :]
            vnew_ref[c * ATT_TK:(c + 1) * ATT_TK, :] = v
            vt_scr[c] = v.T.astype(bf16)

    qt = (_half_rms(q_ref[...], qw_ref[...], bd) * (HALF ** -0.5 * LOG2E)).T
    feat = lax.broadcasted_iota(jnp.int32, qt.shape, 0)
    q2t = jnp.concatenate([jnp.where(feat < HALF, qt, 0.0), jnp.where(feat >= HALF, qt, 0.0)], axis=1).astype(bf16)
    q_groups = [q2t[:, g * ATT_LG:(g + 1) * ATT_LG] for g in range(ATT_NG)]

    def logits(j, groups):
        keys = kn_scr[pl.ds(pl.multiple_of(j * ATT_TK, ATT_TK), ATT_TK), :]
        return [_dot(keys, q_groups[g]) for g in groups]

    def attend(fixed_ref):
        def consume(j, g, s, state, bias):
            m, l, acc, p_prev = state
            acc = acc + _dot(vt_scr[jnp.maximum(j - 1, 0)], p_prev)
            if bias is not None:
                q0 = (g * ATT_LG) % ATT_TQ
                s = s + bias[:, q0:q0 + ATT_LG]
            if fixed_ref is not None:
                p = jnp.exp2(s - fixed_ref)
                return m, l + jnp.sum(p, axis=0, keepdims=True), acc, p.astype(bf16)
            m_new = jnp.maximum(m, jnp.max(s, axis=0, keepdims=True))
            p = jnp.exp2(s - m_new)
            alpha = jnp.exp2(m - m_new)
            return m_new, alpha * l + jnp.sum(p, axis=0, keepdims=True), alpha * acc, p.astype(bf16)

        def step(j, states, bias=None):
            s = logits(j, range(ATT_NG))
            return [consume(j, g, s[g], states[g], bias) for g in range(ATT_NG)]

        states = [(jnp.full((1, ATT_LG), MASKED, f32), jnp.zeros((1, ATT_LG), f32), jnp.zeros((HD, ATT_LG), f32),
                   jnp.zeros((ATT_TK, ATT_LG), bf16)) for _ in range(ATT_NG)]
        n_near = 2 * ATT_TQ // ATT_TK
        near0 = jnp.maximum(qi - 1, 0) * (ATT_TQ // ATT_TK)
        states = lax.fori_loop(0, near0, step, states)
        variant = jnp.minimum(qi, 1)
        for part in range(n_near):
            states = step(near0 + part, states, bias_ref[variant, part * ATT_TK:(part + 1) * ATT_TK, :])
        l = jnp.concatenate([st[1] for st in states], axis=1)
        acc = jnp.concatenate([st[2] + _dot(vt_scr[near0 + n_near - 1], st[3]) for st in states], axis=1)

        lq = lam_ref[...]
        lam = (jnp.exp(jnp.sum(lq[0:1] * lq[1:2], axis=-1, keepdims=True))
               - jnp.exp(jnp.sum(lq[2:3] * lq[3:4], axis=-1, keepdims=True)) + lam_init)
        o2 = acc / l
        o = (o2[:, :ATT_TQ] - lam * o2[:, ATT_TQ:]).T
        o = o * lax.rsqrt(jnp.mean(o * o, axis=-1, keepdims=True) + EPS) * sw_ref[...]
        o_ref[...] = (o * (1.0 - lam_init)).astype(bf16)

    row = layer * H + pl.program_id(1)
    no_underflow = fast_ref[row, 1] > 0.5

    @pl.when(no_underflow)
    def _():
        attend(fast_ref[row, 0])

    @pl.when(jnp.logical_not(no_underflow))
    def _():
        attend(None)


def _softmax_reference(qk_norm, rpb_table):
    wq = jnp.max(jnp.abs(qk_norm[:, 0]), axis=-1)
    wk = jnp.max(jnp.abs(qk_norm[:, 1]), axis=-1)
    bound = (HALF ** 0.5 * LOG2E * wq * wk)[:, None]
    shifted = (rpb_table - rpb_table[FAR_BUCKET]) * LOG2E
    hi, lo = jnp.max(shifted, axis=0)[None], jnp.min(shifted, axis=0)[None]
    flag = (2.0 * bound + hi - lo < MAX_SPAN).astype(f32)
    return jnp.stack([jnp.broadcast_to(bound + hi, flag.shape), flag], axis=-1).reshape(DEPTH * H, 2)


def _attn_prompt_call(p, bias, fast, lam_qk, qk_norm2, subln_w, layer, lam_init):
    nq = SEQ // ATT_TQ
    vec = lambda r: pl.BlockSpec((None, None, 1, HD), lambda b, h, i: (layer, r, 0, 0))
    head_seq = pl.BlockSpec((SEQ, HD), lambda b, h, i: (b, h))
    return pl.pallas_call(
        functools.partial(_attn_prompt_kernel, layer=layer, lam_init=lam_init),
        grid=(BATCH, H, nq),
        in_specs=[
            pl.BlockSpec(memory_space=pltpu.SMEM),
            pl.BlockSpec((None, 4, HALF), lambda b, h, i: (layer, 0, 0)),
            vec(0), vec(1),
            pl.BlockSpec((None, 1, HD), lambda b, h, i: (layer, 0, 0)),
            pl.BlockSpec((None, 2, 2 * ATT_TQ, ATT_TQ), lambda b, h, i: (h, 0, 0, 0)),
            pl.BlockSpec((ATT_TQ, HD), lambda b, h, i: (b * nq + i, h)),
            pl.BlockSpec((SEQ, HD), lambda b, h, i: (b, H + h)),
            pl.BlockSpec((SEQ, HD), lambda b, h, i: (b, 2 * H + h)),
        ],
        out_specs=[pl.BlockSpec((ATT_TQ, HD), lambda b, h, i: (b * nq + i, h)), head_seq, head_seq],
        out_shape=[
            jax.ShapeDtypeStruct((P_TOK, D_ATT), bf16),
            jax.ShapeDtypeStruct((P_TOK, D_ATT), f32),
            jax.ShapeDtypeStruct((P_TOK, D_ATT), f32),
        ],
        scratch_shapes=[pltpu.VMEM((SEQ, HD), bf16), pltpu.VMEM((SEQ // ATT_TK, HD, ATT_TK), bf16)],
        compiler_params=_params(("arbitrary", "arbitrary", "arbitrary"), 56),
        name="diff_attn_prompt",
    )(fast, lam_qk, qk_norm2, qk_norm2, subln_w.reshape(DEPTH, 1, HD), bias, p, p, p)


def _attn_sample_kernel(lam_ref, qw_ref, kw_ref, sw_ref, bias_ref, q_ref, kx_ref, vx_ref, ck_ref, cv_ref,
                        o_ref, knew_ref, vnew_ref, *, lam_init):
    bd = _blockdiag_half()
    n_blk = PAST // SATT_TK
    heads, blocks = range(H), range(n_blk)
    cols = [slice(h * HD, (h + 1) * HD) for h in heads]
    vnew_ref[...] = vx_ref[...]
    q2 = [_two_map_queries(_half_rms(q_ref[:, cols[h]], qw_ref[...], bd) * (HALF ** -0.5)) for h in heads]
    kn = [_half_rms(kx_ref[:, cols[h]], kw_ref[...], bd) for h in heads]
    for h in heads:
        knew_ref[:, cols[h]] = kn[h]

    def cached(ref, h, i):
        return ref[pl.ds(i * SATT_TK * H + h, SATT_TK, stride=H), :].astype(bf16)

    ss = [[_dot_nt(q2[h], cached(ck_ref, h, i)) for i in blocks] for h in heads]
    for h in heads:
        ss[h][-1] = _add_map_bias(ss[h][-1], bias_ref[h, :, :SATT_TK])
        ss[h].append(_add_map_bias(_dot_nt(q2[h], kn[h].astype(bf16)), bias_ref[h, :, SATT_TK:SATT_TK + DEC_SEQ]))
    for h in heads:
        m = functools.reduce(jnp.maximum, [jnp.max(s, axis=-1, keepdims=True) for s in ss[h]])
        ps = [jnp.exp(s - m) for s in ss[h]]
        l = functools.reduce(jnp.add, [jnp.sum(p, axis=-1, keepdims=True) for p in ps])
        vals = [cached(cv_ref, h, i) for i in blocks] + [vx_ref[:, cols[h]].astype(bf16)]
        acc = functools.reduce(jnp.add, [_dot(p.astype(bf16), v) for p, v in zip(ps, vals)])
        o_ref[:, cols[h]] = _diff_combine(l, acc, lam_ref, sw_ref[...], lam_init).astype(bf16)


def _attn_sample_call(p, bias, cache_k, cache_v, lam_qk, qk_norm2, subln_w, layer, lam_init):
    vec = lambda r: pl.BlockSpec((None, None, 1, HD), lambda b: (layer, r, 0, 0))
    sec = lambda c: pl.BlockSpec((DEC_SEQ, D_ATT), lambda b: (b, c))
    cache = pl.BlockSpec((None, None, PAST * H, HD), lambda b: (layer, b, 0, 0))
    stream = pl.BlockSpec((DEC_SEQ, D_ATT), lambda b: (b, 0))
    return pl.pallas_call(
        functools.partial(_attn_sample_kernel, lam_init=lam_init),
        grid=(DEC_BATCH,),
        in_specs=[
            pl.BlockSpec((None, 4, HALF), lambda b: (layer, 0, 0)),
            vec(0), vec(1),
            pl.BlockSpec((None, 1, HD), lambda b: (layer, 0, 0)),
            pl.BlockSpec((H, DEC_SEQ, SATT_TK + HD), lambda b: (0, 0, 0)),
            sec(0), sec(1), sec(2), cache, cache,
        ],
        out_specs=[stream, stream, stream],
        out_shape=[
            jax.ShapeDtypeStruct((S_TOK, D_ATT), bf16),
            jax.ShapeDtypeStruct((S_TOK, D_ATT), f32),
            jax.ShapeDtypeStruct((S_TOK, D_ATT), f32),
        ],
        compiler_params=_params(("arbitrary",), 60),
        name="diff_attn_sample",
    )(lam_qk, qk_norm2, qk_norm2, subln_w.reshape(DEPTH, 1, HD), bias, p, p, p, cache_k, cache_v)


def _gdn_kernel(p_ref, z_ref, ab_ref, hist_ref, s0_ref, cw_ref, alog_ref, dtb_ref, gw_ref,
                o_ref, sfin_ref, cnew_ref, xp_scr, s_scr, *, n_seq):
    per_seq = GDN_SLOTS // n_seq
    rows_seq = per_seq * CHUNK

    @pl.when(pl.program_id(1) == 0)
    def _():
        xp_scr[:, 0:8, :] = hist_ref[...]
        s_scr[...] = s0_ref[...]

    ys = []
    for sq in range(n_seq):
        xp_scr[sq, 8:8 + rows_seq, :] = p_ref[sq * rows_seq:(sq + 1) * rows_seq, :]
        y = xp_scr[sq, 5:5 + rows_seq, :] * cw_ref[0:1, :]
        for i in range(1, CONV_W):
            y = y + xp_scr[sq, 5 + i:5 + i + rows_seq, :] * cw_ref[i:i + 1, :]
        y = _silu(y)
        tail = xp_scr[sq, rows_seq:rows_seq + 8, :]
        xp_scr[sq, 0:8, :] = tail
        cnew_ref[sq] = tail
        ys += [y[c * CHUNK:(c + 1) * CHUNK] for c in range(per_seq)]

    ii = lax.broadcasted_iota(jnp.int32, (CHUNK, CHUNK), 0)
    jj = lax.broadcasted_iota(jnp.int32, (CHUNK, CHUNK), 1)
    causal = ii >= jj
    strict = ii > jj
    eye = jnp.where(ii == jj, 1.0, 0.0)

    units = [(n, h) for n in range(GDN_SLOTS) for h in range(H)]
    q16, k16, kb16, rhs16, gamma, q_dec, k_dec, decay = {}, {}, {}, {}, {}, {}, {}, {}
    for n in range(GDN_SLOTS):
        y = ys[n]
        ab = ab_ref[n * CHUNK:(n + 1) * CHUNK, :]
        pre = ab + dtb_ref[...]
        softplus = jnp.maximum(pre, 0.0) + jnp.log1p(jnp.exp(-jnp.abs(pre)))
        g_all = -jnp.exp(alog_ref[...]) * softplus
        beta_all = jax.nn.sigmoid(ab)
        for h in range(H):
            u = (n, h)
            qx = y[:, h * HD:(h + 1) * HD]
            kx = y[:, D_GDN + h * HD:D_GDN + (h + 1) * HD]
            v = y[:, 2 * D_GDN + h * HD:2 * D_GDN + (h + 1) * HD]
            q = qx * lax.rsqrt(jnp.sum(qx * qx, axis=-1, keepdims=True) + EPS) * (HD ** -0.5)
            k = kx * lax.rsqrt(jnp.sum(kx * kx, axis=-1, keepdims=True) + EPS)
            g = g_all[:, h:h + 1]
            beta = beta_all[:, H + h:H + h + 1]
            gc_row = jnp.sum(jnp.where(ii <= jj, g, 0.0), axis=0, keepdims=True)
            gc = jnp.sum(jnp.where(ii == jj, gc_row, 0.0), axis=1, keepdims=True)
            gc_last = jnp.sum(g, axis=0, keepdims=True)
            gamma[u] = jnp.where(causal, jnp.exp(jnp.where(causal, gc - gc_row, 0.0)), 0.0)
            egc = jnp.exp(gc)
            kb = k * beta
            q16[u], k16[u], kb16[u] = q.astype(bf16), k.astype(bf16), kb.astype(bf16)
            rhs16[u] = jnp.concatenate([kb * egc, v * beta], axis=-1).astype(bf16)
            q_dec[u] = q * egc
            k_dec[u] = k * jnp.exp(gc_last - gc)
            decay[u] = jnp.exp(gc_last)

    kk = {u: _dot_nt(kb16[u], k16[u]) for u in units}
    qk = {u: _dot_nt(q16[u], k16[u]) for u in units}
    a = {u: jnp.where(strict, kk[u] * gamma[u], 0.0) for u in units}
    attn = {u: qk[u] * gamma[u] for u in units}

    base = 8
    d = {u: jnp.where(ii // base == jj // base, a[u], 0.0) for u in units}
    t = {u: eye - d[u] for u in units}
    d16 = {u: d[u].astype(bf16) for u in units}
    pw16 = {u: _dot(d16[u], d16[u]).astype(bf16) for u in units}
    t = {u: t[u] + _dot(t[u].astype(bf16), pw16[u]) for u in units}
    pw16 = {u: _dot(pw16[u], pw16[u]).astype(bf16) for u in units}
    t = {u: t[u] + _dot(t[u].astype(bf16), pw16[u]) for u in units}
    size = base
    while size < CHUNK:
        off = (ii // (2 * size) == jj // (2 * size)) & (ii // size != jj // size)
        t16 = {u: t[u].astype(bf16) for u in units}
        ta = {u: _dot(t16[u], jnp.where(off, a[u], 0.0).astype(bf16)) for u in units}
        t = {u: t[u] - _dot(ta[u].astype(bf16), t16[u]) for u in units}
        size *= 2
    sol = {u: _dot(t[u].astype(bf16), rhs16[u]) for u in units}

    s_cur = {}
    for n in range(GDN_SLOTS):
        sq = n // per_seq
        if n % per_seq == 0:
            s_cur = {h: s_scr[sq, h] for h in range(H)}
        ws = {h: _dot(jnp.concatenate([sol[(n, h)][:, :HD], q_dec[(n, h)]], axis=0).astype(bf16), s_cur[h].astype(bf16))
              for h in range(H)}
        vn16 = {h: (sol[(n, h)][:, HD:] - ws[h][:CHUNK]).astype(bf16) for h in range(H)}
        r = {h: _dot(jnp.concatenate([attn[(n, h)], k_dec[(n, h)].T], axis=0).astype(bf16), vn16[h]) for h in range(H)}
        for h in range(H):
            cols = slice(h * HD, (h + 1) * HD)
            rows = slice(n * CHUNK, (n + 1) * CHUNK)
            s_cur[h] = s_cur[h] * decay[(n, h)] + r[h][CHUNK:]
            o = ws[h][CHUNK:] + r[h][:CHUNK]
            o = o * lax.rsqrt(jnp.mean(o * o, axis=-1, keepdims=True) + EPS) * gw_ref[...]
            o_ref[rows, cols] = (o * _silu(z_ref[rows, cols])).astype(bf16)
        if n % per_seq == per_seq - 1:
            for h in range(H):
                s_scr[sq, h] = s_cur[h]

    @pl.when(pl.program_id(1) == pl.num_programs(1) - 1)
    def _():
        sfin_ref[...] = s_scr[...]


def _gdn_call(p, hist, s0, conv_w, a_log, dt_bias, gdn_norm_w, layer, n_blocks, steps, n_seq):
    br = GDN_SLOTS * CHUNK
    rows = lambda col, width: pl.BlockSpec((br, width), lambda b, c: (b * steps + c, col))
    vec = pl.BlockSpec((None, 1, HD), lambda b, c: (layer, 0, 0))
    return pl.pallas_call(
        functools.partial(_gdn_kernel, n_seq=n_seq),
        grid=(n_blocks, steps),
        in_specs=[
            rows(COL_CONV // CONV_CH, CONV_CH),
            rows(COL_Z // D_GDN, D_GDN),
            rows(COL_AB // HD, HD),
            pl.BlockSpec((n_seq, 8, CONV_CH), lambda b, c: (b, 0, 0)),
            pl.BlockSpec((n_seq, H, HD, HD), lambda b, c: (b, 0, 0, 0)),
            pl.BlockSpec((None, CONV_W, CONV_CH), lambda b, c: (layer, 0, 0)),
            vec, vec, vec,
        ],
        out_specs=[
            pl.BlockSpec((br, D_GDN), lambda b, c: (b * steps + c, 0)),
            pl.BlockSpec((n_seq, H, HD, HD), lambda b, c: (b, 0, 0, 0)),
            pl.BlockSpec((n_seq, 8, CONV_CH), lambda b, c: (b, 0, 0)),
        ],
        out_shape=[
            jax.ShapeDtypeStruct((n_blocks * steps * br, D_GDN), bf16),
            jax.ShapeDtypeStruct((n_blocks * n_seq, H, HD, HD), f32),
            jax.ShapeDtypeStruct((n_blocks * n_seq, 8, CONV_CH), f32),
        ],
        scratch_shapes=[pltpu.VMEM((n_seq, 8 + br // n_seq, CONV_CH), f32), pltpu.VMEM((n_seq, H, HD, HD), f32)],
        compiler_params=_params(("arbitrary", "arbitrary"), 48),
        name="gated_deltanet",
    )(p, p, p, hist, s0, conv_w, a_log, dt_bias, gdn_norm_w)


def _pad_lanes(v):
    return jnp.pad(v, ((0, 0), (0, HD - v.shape[-1]))).reshape(DEPTH, 1, HD)


def kernel(x_prompt, x_sample, cache_k, cache_v, state_gdn, state_conv, c_prompt, c_sample, w_mod, b_mod, norm_g, ffn_up, ffn_down, w_in, w_out, qk_norm, lam_qk, subln_w, rpb_table, conv_w, a_log, dt_bias, gdn_norm_w):
    xp = x_prompt.reshape(P_TOK, D)
    xs = x_sample.reshape(S_TOK, D)
    c_all = jnp.concatenate([c_prompt, c_sample, jnp.zeros((N_CBATCH - BATCH - DEC_BATCH, D), f32)], axis=0)

    w_up16 = ffn_up.astype(bf16)
    w_down16 = ffn_down.astype(bf16)
    w_in16 = jnp.pad(w_in.astype(bf16), ((0, 0), (0, 0), (0, N_PROJ - N_IN)))
    w_out16 = w_out.astype(bf16)
    norm_g4 = norm_g.reshape(DEPTH, 3, 1, D)
    qk_norm2 = jnp.concatenate([qk_norm, qk_norm], axis=-1).reshape(DEPTH, 2, 1, HD)
    gdn_w = gdn_norm_w.reshape(DEPTH, 1, HD)
    a_log_p, dt_bias_p = _pad_lanes(a_log), _pad_lanes(dt_bias)
    ck = cache_k.reshape(DEPTH, DEC_BATCH, PAST * H, HD)
    cv = cache_v.reshape(DEPTH, DEC_BATCH, PAST * H, HD)
    hist_s = jnp.pad(state_conv, ((0, 0), (0, 0), (8 - (CONV_W - 1), 0), (0, 0)))
    hist_p = jnp.zeros((BATCH, 8, CONV_CH), f32)
    s0_p = jnp.zeros((BATCH, H, HD, HD), f32)

    mod = _mod_call(c_all, w_mod, b_mod).reshape(DEPTH, N_CBATCH, N_MOD, D)
    modg_p = jnp.transpose(jnp.repeat(mod[:, :BATCH], SEQ // GROUP, axis=1), (0, 2, 1, 3))
    modg_s = jnp.transpose(mod[:, BATCH:BATCH + DEC_BATCH], (0, 2, 1, 3))

    pbias, sbias = _bias_call(rpb_table)
    fast = _softmax_reference(qk_norm, rpb_table)

    gdn_rows = GDN_SLOTS * CHUNK
    outs = {n: [] for n in ("kp", "vp", "sp", "cp", "ks", "vs", "ss", "cs")}
    for l in range(DEPTH):
        lam_init = 0.8 - 0.6 * math.exp(-0.3 * l)
        xp = _ffn_call(xp, modg_p[l], norm_g4, w_up16, w_down16, l, 0, 0)
        xs = _ffn_call(xs, modg_s[l], norm_g4, w_up16, w_down16, l, 0, 0)
        pp = _proj_call(xp, modg_p[l], norm_g4, w_in16, l)
        ps = _proj_call(xs, modg_s[l], norm_g4, w_in16, l)

        oa_p, k_p, v_p = _attn_prompt_call(pp, pbias, fast, lam_qk, qk_norm2, subln_w, l, lam_init)
        oa_s, k_s, v_s = _attn_sample_call(ps, sbias, ck, cv, lam_qk, qk_norm2, subln_w, l, lam_init)
        og_p, s_p, c_p = _gdn_call(pp, hist_p, s0_p, conv_w, a_log_p, dt_bias_p, gdn_w, l, BATCH, SEQ // gdn_rows, 1)
        og_s, s_s, c_s = _gdn_call(ps, hist_s[l], state_gdn[l], conv_w, a_log_p, dt_bias_p, gdn_w, l,
                                   DEC_BATCH // GDN_SLOTS, 1, GDN_SLOTS)

        xp = _out_call(xp, modg_p[l], oa_p, og_p, w_out16, l)
        xs = _out_call(xs, modg_s[l], oa_s, og_s, w_out16, l)
        xp = _ffn_call(xp, modg_p[l], norm_g4, w_up16, w_down16, l, 1, 2)
        xs = _ffn_call(xs, modg_s[l], norm_g4, w_up16, w_down16, l, 1, 2)

        outs["kp"].append(k_p.reshape(BATCH, SEQ, H, HD))
        outs["vp"].append(v_p.reshape(BATCH, SEQ, H, HD))
        outs["sp"].append(s_p)
        outs["cp"].append(c_p[:, 8 - (CONV_W - 1):])
        outs["ks"].append(k_s.reshape(DEC_BATCH, DEC_SEQ, H, HD))
        outs["vs"].append(v_s.reshape(DEC_BATCH, DEC_SEQ, H, HD))
        outs["ss"].append(s_s)
        outs["cs"].append(c_s[:, 8 - (CONV_W - 1):])

    st = {n: jnp.stack(v) for n, v in outs.items()}
    return (xp.reshape(BATCH, SEQ, D), xs.reshape(DEC_BATCH, DEC_SEQ, D),
            st["kp"], st["vp"], st["sp"], st["cp"], st["ks"], st["vs"], st["ss"], st["cs"])
```

```python
import functools
import math

import jax
import jax.numpy as jnp
from jax import lax
from jax.experimental import pallas as pl
from jax.experimental.pallas import tpu as pltpu

f32 = jnp.float32
bf16 = jnp.bfloat16

D = 2048
BATCH = 2
SEQ = 4096
DEPTH = 2
DEC_BATCH = 32
DEC_SEQ = 64
PAST = 2048
CHUNK = 64
H = 8
HD = 128
HALF = 64
D_ATT = H * HD
D_GDN = H * HD
CONV_W = 4
CONV_CH = 3 * D_GDN
D_FF = 5632
NUM_BUCKETS = 32
FAR_BUCKET = 15
FAR_DIST = 91
N_MOD = 9
N_IN = 3 * D_ATT + CONV_CH + D_GDN + 2 * H
EPS = 1e-6

GROUP = 64
P_TOK = BATCH * SEQ
S_TOK = DEC_BATCH * DEC_SEQ
N_CBATCH = 40
N_PROJ = 7680
COL_K, COL_V, COL_CONV, COL_Z, COL_AB = D_ATT, 2 * D_ATT, 3 * D_ATT, 3 * D_ATT + CONV_CH, 3 * D_ATT + CONV_CH + D_GDN

MASKED = -1e30
SUB = 128

FFN_TM, FFN_FC = 512, 512
PROJ_TM, PROJ_TN = 1024, 512
OUT_TM = 512
MOD_TN = 1024
ATT_TQ = 512
ATT_TK = 512
ATT_NG = 4
ATT_LG = 2 * ATT_TQ // ATT_NG
LOG2E = math.log2(math.e)
MAX_SPAN = 100.0
SATT_TK = 512
GDN_SLOTS = 2


def _params(sem, vmem_mb):
    return pltpu.CompilerParams(dimension_semantics=sem, vmem_limit_bytes=vmem_mb * 1024 * 1024)


def _dot(a, b):
    return jnp.dot(a, b, preferred_element_type=f32)


def _dot_nt(a, b):
    return lax.dot_general(a, b, (((1,), (1,)), ((), ())), preferred_element_type=f32)


def _silu(x):
    return x * jax.nn.sigmoid(x)


def _modulated_norm(x, ng, scale, shift):
    tm = x.shape[0]
    ms = jnp.mean(x * x, axis=-1, keepdims=True)
    y = x * lax.rsqrt(ms + EPS) * ng
    y = y.reshape(tm // GROUP, GROUP, D) * (1.0 + scale[:, None, :]) + shift[:, None, :]
    return y.reshape(tm, D)


def _gated(y, gate):
    tm = y.shape[0]
    return (y.reshape(tm // GROUP, GROUP, D) * gate[:, None, :]).reshape(tm, D)


def _blockdiag_half():
    r = lax.broadcasted_iota(jnp.int32, (HD, HD), 0) // HALF
    c = lax.broadcasted_iota(jnp.int32, (HD, HD), 1) // HALF
    return (r == c).astype(bf16)


def _half_rms(x, w, bd):
    ss = _dot((x * x).astype(bf16), bd) * (1.0 / HALF)
    return x * lax.rsqrt(ss + EPS) * w


def _mod_kernel(c_ref, w_ref, b_ref, o_ref):
    a = _silu(c_ref[...]).astype(bf16)
    o_ref[...] = _dot(a, w_ref[...].astype(bf16)) + b_ref[...]


def _mod_call(c_all, w_mod, b_mod):
    n = N_MOD * D // MOD_TN
    return pl.pallas_call(
        _mod_kernel,
        grid=(DEPTH, n),
        in_specs=[
            pl.BlockSpec((N_CBATCH, D), lambda l, j: (0, 0)),
            pl.BlockSpec((None, D, MOD_TN), lambda l, j: (l, 0, j)),
            pl.BlockSpec((None, 1, MOD_TN), lambda l, j: (l, 0, j)),
        ],
        out_specs=pl.BlockSpec((None, N_CBATCH, MOD_TN), lambda l, j: (l, 0, j)),
        out_shape=jax.ShapeDtypeStruct((DEPTH, N_CBATCH, N_MOD * D), f32),
        compiler_params=_params(("arbitrary", "arbitrary"), 40),
        name="adaln_mod",
    )(c_all, w_mod, b_mod.reshape(DEPTH, 1, N_MOD * D))


def _ffn_kernel(x_ref, xn_ref, shift0_ref, scale0_ref, shiftn_ref, scalen_ref, gate_ref, ng_ref,
                wg_ref, wu_ref, wd_ref, o_ref, h_scr, acc_scr):
    m, k = pl.program_id(0), pl.program_id(1)
    slot = m % 2

    @pl.when((m == 0) & (k == 0))
    def _():
        h_scr[0] = _modulated_norm(x_ref[...], ng_ref[...], scale0_ref[...], shift0_ref[...]).astype(bf16)

    def ff_chunk():
        h = h_scr[slot]
        act = (_silu(_dot(h, wg_ref[...])) * _dot(h, wu_ref[...])).astype(bf16)
        acc_scr[...] = jnp.where(k == 0, 0.0, acc_scr[...]) + _dot(act, wd_ref[...])

    @pl.when(k < pl.num_programs(1) - 1)
    def _():
        ff_chunk()

    @pl.when(k == pl.num_programs(1) - 1)
    def _():
        ff_chunk()
        h_scr[1 - slot] = _modulated_norm(xn_ref[...], ng_ref[...], scalen_ref[...], shiftn_ref[...]).astype(bf16)
        o_ref[...] = x_ref[...] + 0.5 * _gated(acc_scr[...], gate_ref[...])


def _ffn_call(x, modg, norm_g, w_up, w_down, layer, idx, j):
    n_tok = x.shape[0]
    nm = n_tok // FFN_TM
    nk = D_FF // FFN_FC
    gm = FFN_TM // GROUP
    nxt = lambda m: jnp.minimum(m + 1, nm - 1)
    return pl.pallas_call(
        _ffn_kernel,
        grid=(nm, nk),
        in_specs=[
            pl.BlockSpec((FFN_TM, D), lambda m, k: (m, 0)),
            pl.BlockSpec((FFN_TM, D), lambda m, k: (nxt(m), 0)),
            pl.BlockSpec((None, gm, D), lambda m, k: (3 * j, 0, 0)),
            pl.BlockSpec((None, gm, D), lambda m, k: (3 * j + 1, 0, 0)),
            pl.BlockSpec((None, gm, D), lambda m, k: (3 * j, nxt(m), 0)),
            pl.BlockSpec((None, gm, D), lambda m, k: (3 * j + 1, nxt(m), 0)),
            pl.BlockSpec((None, gm, D), lambda m, k: (3 * j + 2, m, 0)),
            pl.BlockSpec((None, None, 1, D), lambda m, k: (layer, j, 0, 0)),
            pl.BlockSpec((None, None, D, FFN_FC), lambda m, k: (layer, idx, 0, k)),
            pl.BlockSpec((None, None, D, FFN_FC), lambda m, k: (layer, idx, 0, nk + k)),
            pl.BlockSpec((None, None, FFN_FC, D), lambda m, k: (layer, idx, k, 0)),
        ],
        out_specs=pl.BlockSpec((FFN_TM, D), lambda m, k: (m, 0)),
        out_shape=jax.ShapeDtypeStruct((n_tok, D), f32),
        scratch_shapes=[pltpu.VMEM((2, FFN_TM, D), bf16), pltpu.VMEM((FFN_TM, D), f32)],
        compiler_params=_params(("arbitrary", "arbitrary"), 56),
        name="swiglu_half_step",
    )(x, x, modg, modg, modg, modg, modg, norm_g, w_up, w_up, w_down)


def _proj_kernel(x_ref, shift_ref, scale_ref, ng_ref, w_ref, o_ref, h_scr):
    @pl.when(pl.program_id(1) == 0)
    def _():
        h_scr[...] = _modulated_norm(x_ref[...], ng_ref[...], scale_ref[...], shift_ref[...]).astype(bf16)

    o_ref[...] = _dot(h_scr[...], w_ref[...])


def _proj_call(x, modg, norm_g, w_in, layer):
    n_tok = x.shape[0]
    gm = PROJ_TM // GROUP
    row = lambda r: pl.BlockSpec((None, gm, D), lambda m, n: (r, m, 0))
    return pl.pallas_call(
        _proj_kernel,
        grid=(n_tok // PROJ_TM, N_PROJ // PROJ_TN),
        in_specs=[
            pl.BlockSpec((PROJ_TM, D), lambda m, n: (m, 0)),
            row(3), row(4),
            pl.BlockSpec((None, None, 1, D), lambda m, n: (layer, 1, 0, 0)),
            pl.BlockSpec((None, D, PROJ_TN), lambda m, n: (layer, 0, n)),
        ],
        out_specs=pl.BlockSpec((PROJ_TM, PROJ_TN), lambda m, n: (m, n)),
        out_shape=jax.ShapeDtypeStruct((n_tok, N_PROJ), f32),
        scratch_shapes=[pltpu.VMEM((PROJ_TM, D), bf16)],
        compiler_params=_params(("arbitrary", "arbitrary"), 40),
        name="mixer_in_proj",
    )(x, modg, modg, norm_g, w_in)


def _out_kernel(x_ref, gate_ref, a_ref, g_ref, w_ref, o_ref):
    y = _dot(a_ref[...], w_ref[:D_ATT, :]) + _dot(g_ref[...], w_ref[D_ATT:, :])
    o_ref[...] = x_ref[...] + _gated(y, gate_ref[...])


def _out_call(x, modg, o_att, o_gdn, w_out, layer):
    n_tok = x.shape[0]
    gm = OUT_TM // GROUP
    return pl.pallas_call(
        _out_kernel,
        grid=(n_tok // OUT_TM,),
        in_specs=[
            pl.BlockSpec((OUT_TM, D), lambda m: (m, 0)),
            pl.BlockSpec((None, gm, D), lambda m: (5, m, 0)),
            pl.BlockSpec((OUT_TM, D_ATT), lambda m: (m, 0)),
            pl.BlockSpec((OUT_TM, D_GDN), lambda m: (m, 0)),
            pl.BlockSpec((None, D, D), lambda m: (layer, 0, 0)),
        ],
        out_specs=pl.BlockSpec((OUT_TM, D), lambda m: (m, 0)),
        out_shape=jax.ShapeDtypeStruct((n_tok, D), f32),
        compiler_params=_params(("arbitrary",), 48),
        name="mixer_out_proj",
    )(x, modg, o_att, o_gdn, w_out)


def _bias_from_rel(rel, table_ref, h):
    n = jnp.abs(rel)
    big = jnp.full_like(n, 8)
    for t in (12, 16, 23, 32, 46, 64, FAR_DIST):
        big = big + (n >= t).astype(jnp.int32)
    bucket = jnp.where(n < 8, n, big) + jnp.where(rel > 0, NUM_BUCKETS // 2, 0)
    out = jnp.zeros(rel.shape, f32)
    for b in range(NUM_BUCKETS):
        out = jnp.where(bucket == b, table_ref[b, h], out)
    return out


def _fill_bias(dst_ref, lead, n_rows, n_cols, col0, key_axis, key_base, table_ref, h, scale=1.0):
    far = table_ref[FAR_BUCKET, h]
    for r0 in range(0, n_rows, SUB):
        nr = min(SUB, n_rows - r0)
        for c0 in range(0, n_cols, SUB):
            r = lax.broadcasted_iota(jnp.int32, (nr, SUB), 0) + r0
            c = lax.broadcasted_iota(jnp.int32, (nr, SUB), 1) + c0
            key, query = (r + key_base, c) if key_axis == 0 else (c + key_base, r)
            k_lo, q_lo = (r0 + key_base, c0) if key_axis == 0 else (c0 + key_base, r0)
            k_hi = k_lo + (nr if key_axis == 0 else SUB) - 1
            q_hi = q_lo + (SUB if key_axis == 0 else nr) - 1
            if k_lo // CHUNK > q_hi // CHUNK:
                tile = jnp.full((nr, SUB), MASKED, f32)
            elif k_hi - q_lo <= -FAR_DIST:
                tile = jnp.zeros((nr, SUB), f32)
            else:
                tile = (_bias_from_rel(key - query, table_ref, h) - far) * scale
                tile = jnp.where(key // CHUNK <= query // CHUNK, tile, MASKED)
            dst_ref[lead + (slice(r0, r0 + nr), slice(col0 + c0, col0 + c0 + SUB))] = tile


def _bias_kernel(table_ref, pb_ref, sb_ref):
    h = pl.program_id(0)
    _fill_bias(pb_ref, (0,), 2 * ATT_TQ, ATT_TQ, 0, 0, 0, table_ref, h, LOG2E)
    _fill_bias(pb_ref, (1,), 2 * ATT_TQ, ATT_TQ, 0, 0, -ATT_TQ, table_ref, h, LOG2E)
    _fill_bias(sb_ref, (), DEC_SEQ, SATT_TK, 0, 1, -SATT_TK, table_ref, h)
    _fill_bias(sb_ref, (), DEC_SEQ, HD, SATT_TK, 1, 0, table_ref, h)


def _bias_call(rpb_table):
    return pl.pallas_call(
        _bias_kernel,
        grid=(H,),
        in_specs=[pl.BlockSpec(memory_space=pltpu.SMEM)],
        out_specs=[
            pl.BlockSpec((None, 2, 2 * ATT_TQ, ATT_TQ), lambda h: (h, 0, 0, 0)),
            pl.BlockSpec((None, DEC_SEQ, SATT_TK + HD), lambda h: (h, 0, 0)),
        ],
        out_shape=[
            jax.ShapeDtypeStruct((H, 2, 2 * ATT_TQ, ATT_TQ), f32),
            jax.ShapeDtypeStruct((H, DEC_SEQ, SATT_TK + HD), f32),
        ],
        compiler_params=_params(("arbitrary",), 32),
        name="rel_pos_bias_tiles",
    )(rpb_table)


def _two_map_queries(qn):
    lane = lax.broadcasted_iota(jnp.int32, qn.shape, 1)
    return jnp.concatenate([jnp.where(lane < HALF, qn, 0.0), jnp.where(lane >= HALF, qn, 0.0)], axis=0).astype(bf16)


def _add_map_bias(s, b):
    n = b.shape[0]
    return (s.reshape(2, n, s.shape[1]) + b[None]).reshape(s.shape)


def _diff_combine(l, acc, lam_ref, sw, lam_init):
    n = acc.shape[0] // 2
    lq = lam_ref[...]
    lam = (jnp.exp(jnp.sum(lq[0:1] * lq[1:2], axis=-1, keepdims=True))
           - jnp.exp(jnp.sum(lq[2:3] * lq[3:4], axis=-1, keepdims=True)) + lam_init)
    o = acc[:n] / l[:n] - lam * (acc[n:] / l[n:])
    o = o * lax.rsqrt(jnp.mean(o * o, axis=-1, keepdims=True) + EPS) * sw
    return o * (1.0 - lam_init)


def _attn_prompt_kernel(fast_ref, lam_ref, qw_ref, kw_ref, sw_ref, bias_ref, q_ref, k_ref, v_ref,
                        o_ref, knew_ref, vnew_ref, kn_scr, vt_scr, *, layer, lam_init):
    qi = pl.program_id(2)
    bd = _blockdiag_half()

    @pl.when(qi == 0)
    def _():
        kn = _half_rms(k_ref[...], kw_ref[...], bd)
        knew_ref[...] = kn
        kn_scr[...] = kn.astype(bf16)
        for c in range(SEQ // ATT_TK):
            v = v_ref[c * ATT_TK:(c + 1) * ATT_TK, :]
            vnew_ref[c * ATT_TK:(c + 1) * ATT_TK, :] = v
            vt_scr[c] = v.T.astype(bf16)

    qt = (_half_rms(q_ref[...], qw_ref[...], bd) * (HALF ** -0.5 * LOG2E)).T
    feat = lax.broadcasted_iota(jnp.int32, qt.shape, 0)
    q2t = jnp.concatenate([jnp.where(feat < HALF, qt, 0.0), jnp.where(feat >= HALF, qt, 0.0)], axis=1).astype(bf16)
    q_groups = [q2t[:, g * ATT_LG:(g + 1) * ATT_LG] for g in range(ATT_NG)]

    def logits(j, groups):
        keys = kn_scr[pl.ds(pl.multiple_of(j * ATT_TK, ATT_TK), ATT_TK), :]
        return [_dot(keys, q_groups[g]) for g in groups]

    def attend(fixed_ref):
        def consume(j, g, s, state, bias):
            m, l, acc, p_prev = state
            acc = acc + _dot(vt_scr[jnp.maximum(j - 1, 0)], p_prev)
            if bias is not None:
                q0 = (g * ATT_LG) % ATT_TQ
                s = s + bias[:, q0:q0 + ATT_LG]
            if fixed_ref is not None:
                p = jnp.exp2(s - fixed_ref)
                return m, l + jnp.sum(p, axis=0, keepdims=True), acc, p.astype(bf16)
            m_new = jnp.maximum(m, jnp.max(s, axis=0, keepdims=True))
            p = jnp.exp2(s - m_new)
            alpha = jnp.exp2(m - m_new)
            return m_new, alpha * l + jnp.sum(p, axis=0, keepdims=True), alpha * acc, p.astype(bf16)

        def step(j, states, bias=None):
            s = logits(j, range(ATT_NG))
            return [consume(j, g, s[g], states[g], bias) for g in range(ATT_NG)]

        states = [(jnp.full((1, ATT_LG), MASKED, f32), jnp.zeros((1, ATT_LG), f32), jnp.zeros((HD, ATT_LG), f32),
                   jnp.zeros((ATT_TK, ATT_LG), bf16)) for _ in range(ATT_NG)]
        n_near = 2 * ATT_TQ // ATT_TK
        near0 = jnp.maximum(qi - 1, 0) * (ATT_TQ // ATT_TK)
        states = lax.fori_loop(0, near0, step, states)
        variant = jnp.minimum(qi, 1)
        for part in range(n_near):
            states = step(near0 + part, states, bias_ref[variant, part * ATT_TK:(part + 1) * ATT_TK, :])
        l = jnp.concatenate([st[1] for st in states], axis=1)
        acc = jnp.concatenate([st[2] + _dot(vt_scr[near0 + n_near - 1], st[3]) for st in states], axis=1)

        lq = lam_ref[...]
        lam = (jnp.exp(jnp.sum(lq[0:1] * lq[1:2], axis=-1, keepdims=True))
               - jnp.exp(jnp.sum(lq[2:3] * lq[3:4], axis=-1, keepdims=True)) + lam_init)
        o2 = acc / l
        o = (o2[:, :ATT_TQ] - lam * o2[:, ATT_TQ:]).T
        o = o * lax.rsqrt(jnp.mean(o * o, axis=-1, keepdims=True) + EPS) * sw_ref[...]
        o_ref[...] = (o * (1.0 - lam_init)).astype(bf16)

    row = layer * H + pl.program_id(1)
    no_underflow = fast_ref[row, 1] > 0.5

    @pl.when(no_underflow)
    def _():
        attend(fast_ref[row, 0])

    @pl.when(jnp.logical_not(no_underflow))
    def _():
        attend(None)


def _softmax_reference(qk_norm, rpb_table):
    wq = jnp.max(jnp.abs(qk_norm[:, 0]), axis=-1)
    wk = jnp.max(jnp.abs(qk_norm[:, 1]), axis=-1)
    bound = (HALF ** 0.5 * LOG2E * wq * wk)[:, None]
    shifted = (rpb_table - rpb_table[FAR_BUCKET]) * LOG2E
    hi, lo = jnp.max(shifted, axis=0)[None], jnp.min(shifted, axis=0)[None]
    flag = (2.0 * bound + hi - lo < MAX_SPAN).astype(f32)
    return jnp.stack([jnp.broadcast_to(bound + hi, flag.shape), flag], axis=-1).reshape(DEPTH * H, 2)


def _attn_prompt_call(p, bias, fast, lam_qk, qk_norm2, subln_w, layer, lam_init):
    nq = SEQ // ATT_TQ
    vec = lambda r: pl.BlockSpec((None, None, 1, HD), lambda b, h, i: (layer, r, 0, 0))
    head_seq = pl.BlockSpec((SEQ, HD), lambda b, h, i: (b, h))
    return pl.pallas_call(
        functools.partial(_attn_prompt_kernel, layer=layer, lam_init=lam_init),
        grid=(BATCH, H, nq),
        in_specs=[
            pl.BlockSpec(memory_space=pltpu.SMEM),
            pl.BlockSpec((None, 4, HALF), lambda b, h, i: (layer, 0, 0)),
            vec(0), vec(1),
            pl.BlockSpec((None, 1, HD), lambda b, h, i: (layer, 0, 0)),
            pl.BlockSpec((None, 2, 2 * ATT_TQ, ATT_TQ), lambda b, h, i: (h, 0, 0, 0)),
            pl.BlockSpec((ATT_TQ, HD), lambda b, h, i: (b * nq + i, h)),
            pl.BlockSpec((SEQ, HD), lambda b, h, i: (b, H + h)),
            pl.BlockSpec((SEQ, HD), lambda b, h, i: (b, 2 * H + h)),
        ],
        out_specs=[pl.BlockSpec((ATT_TQ, HD), lambda b, h, i: (b * nq + i, h)), head_seq, head_seq],
        out_shape=[
            jax.ShapeDtypeStruct((P_TOK, D_ATT), bf16),
            jax.ShapeDtypeStruct((P_TOK, D_ATT), f32),
            jax.ShapeDtypeStruct((P_TOK, D_ATT), f32),
        ],
        scratch_shapes=[pltpu.VMEM((SEQ, HD), bf16), pltpu.VMEM((SEQ // ATT_TK, HD, ATT_TK), bf16)],
        compiler_params=_params(("arbitrary", "arbitrary", "arbitrary"), 56),
        name="diff_attn_prompt",
    )(fast, lam_qk, qk_norm2, qk_norm2, subln_w.reshape(DEPTH, 1, HD), bias, p, p, p)


def _attn_sample_kernel(lam_ref, qw_ref, kw_ref, sw_ref, bias_ref, q_ref, kx_ref, vx_ref, ck_ref, cv_ref,
                        o_ref, knew_ref, vnew_ref, *, lam_init):
    bd = _blockdiag_half()
    n_blk = PAST // SATT_TK
    heads, blocks = range(H), range(n_blk)
    cols = [slice(h * HD, (h + 1) * HD) for h in heads]
    vnew_ref[...] = vx_ref[...]
    q2 = [_two_map_queries(_half_rms(q_ref[:, cols[h]], qw_ref[...], bd) * (HALF ** -0.5)) for h in heads]
    kn = [_half_rms(kx_ref[:, cols[h]], kw_ref[...], bd) for h in heads]
    for h in heads:
        knew_ref[:, cols[h]] = kn[h]

    def cached(ref, h, i):
        return ref[pl.ds(i * SATT_TK * H + h, SATT_TK, stride=H), :].astype(bf16)

    ss = [[_dot_nt(q2[h], cached(ck_ref, h, i)) for i in blocks] for h in heads]
    for h in heads:
        ss[h][-1] = _add_map_bias(ss[h][-1], bias_ref[h, :, :SATT_TK])
        ss[h].append(_add_map_bias(_dot_nt(q2[h], kn[h].astype(bf16)), bias_ref[h, :, SATT_TK:SATT_TK + DEC_SEQ]))
    for h in heads:
        m = functools.reduce(jnp.maximum, [jnp.max(s, axis=-1, keepdims=True) for s in ss[h]])
        ps = [jnp.exp(s - m) for s in ss[h]]
        l = functools.reduce(jnp.add, [jnp.sum(p, axis=-1, keepdims=True) for p in ps])
        vals = [cached(cv_ref, h, i) for i in blocks] + [vx_ref[:, cols[h]].astype(bf16)]
        acc = functools.reduce(jnp.add, [_dot(p.astype(bf16), v) for p, v in zip(ps, vals)])
        o_ref[:, cols[h]] = _diff_combine(l, acc, lam_ref, sw_ref[...], lam_init).astype(bf16)


def _attn_sample_call(p, bias, cache_k, cache_v, lam_qk, qk_norm2, subln_w, layer, lam_init):
    vec = lambda r: pl.BlockSpec((None, None, 1, HD), lambda b: (layer, r, 0, 0))
    sec = lambda c: pl.BlockSpec((DEC_SEQ, D_ATT), lambda b: (b, c))
    cache = pl.BlockSpec((None, None, PAST * H, HD), lambda b: (layer, b, 0, 0))
    stream = pl.BlockSpec((DEC_SEQ, D_ATT), lambda b: (b, 0))
    return pl.pallas_call(
        functools.partial(_attn_sample_kernel, lam_init=lam_init),
        grid=(DEC_BATCH,),
        in_specs=[
            pl.BlockSpec((None, 4, HALF), lambda b: (layer, 0, 0)),
            vec(0), vec(1),
            pl.BlockSpec((None, 1, HD), lambda b: (layer, 0, 0)),
            pl.BlockSpec((H, DEC_SEQ, SATT_TK + HD), lambda b: (0, 0, 0)),
            sec(0), sec(1), sec(2), cache, cache,
        ],
        out_specs=[stream, stream, stream],
        out_shape=[
            jax.ShapeDtypeStruct((S_TOK, D_ATT), bf16),
            jax.ShapeDtypeStruct((S_TOK, D_ATT), f32),
            jax.ShapeDtypeStruct((S_TOK, D_ATT), f32),
        ],
        compiler_params=_params(("arbitrary",), 60),
        name="diff_attn_sample",
    )(lam_qk, qk_norm2, qk_norm2, subln_w.reshape(DEPTH, 1, HD), bias, p, p, p, cache_k, cache_v)


def _gdn_kernel(p_ref, z_ref, ab_ref, hist_ref, s0_ref, cw_ref, alog_ref, dtb_ref, gw_ref,
                o_ref, sfin_ref, cnew_ref, xp_scr, s_scr, *, n_seq):
    per_seq = GDN_SLOTS // n_seq
    rows_seq = per_seq * CHUNK

    @pl.when(pl.program_id(1) == 0)
    def _():
        xp_scr[:, 0:8, :] = jnp.zeros((n_seq, 8, CONV_CH), f32)
        xp_scr[:, 8:16, :] = hist_ref[...]
        s_scr[...] = s0_ref[...]

    ys = []
    for sq in range(n_seq):
        xp_scr[sq, 16:16 + rows_seq, :] = p_ref[sq * rows_seq:(sq + 1) * rows_seq, :]
        x_cur = xp_scr[sq, 8:16 + rows_seq, :]
        x_prev = xp_scr[sq, 7:15 + rows_seq, :]
        pair_lo = x_cur * cw_ref[1:2, :] + x_prev * cw_ref[0:1, :]
        pair_hi = x_cur[8:] * cw_ref[3:4, :] + x_prev[8:] * cw_ref[2:3, :]
        y = _silu(pair_hi + pair_lo[6:6 + rows_seq])
        tail = xp_scr[sq, 8 + rows_seq:16 + rows_seq, :]
        xp_scr[sq, 8:16, :] = tail
        cnew_ref[sq] = tail
        ys += [y[c * CHUNK:(c + 1) * CHUNK] for c in range(per_seq)]

    ii = lax.broadcasted_iota(jnp.int32, (CHUNK, CHUNK), 0)
    jj = lax.broadcasted_iota(jnp.int32, (CHUNK, CHUNK), 1)
    causal = ii >= jj
    strict = ii > jj
    eye = jnp.where(ii == jj, 1.0, 0.0)
    ones = jnp.ones((HD, HD), bf16)

    units = [(n, h) for n in range(GDN_SLOTS) for h in range(H)]
    q16, k16, kb16, rhs16, gamma, q_dec, k_dec, decay = {}, {}, {}, {}, {}, {}, {}, {}
    for n in range(GDN_SLOTS):
        y = ys[n]
        ab = ab_ref[n * CHUNK:(n + 1) * CHUNK, :]
        pre = ab + dtb_ref[...]
        softplus = jnp.maximum(pre, 0.0) + jnp.log1p(jnp.exp(-jnp.abs(pre)))
        g_all = -jnp.exp(alog_ref[...]) * softplus
        beta_all = jax.nn.sigmoid(ab)
        for h in range(H):
            u = (n, h)
            qx = y[:, h * HD:(h + 1) * HD]
            kx = y[:, D_GDN + h * HD:D_GDN + (h + 1) * HD]
            v = y[:, 2 * D_GDN + h * HD:2 * D_GDN + (h + 1) * HD]
            q = qx * lax.rsqrt(_dot((qx * qx).astype(bf16), ones) + EPS) * (HD ** -0.5)
            k = kx * lax.rsqrt(_dot((kx * kx).astype(bf16), ones) + EPS)
            g = g_all[:, h:h + 1]
            beta = beta_all[:, H + h:H + h + 1]
            gc_row = jnp.sum(jnp.where(ii <= jj, g, 0.0), axis=0, keepdims=True)
            gc = jnp.sum(jnp.where(ii == jj, gc_row, 0.0), axis=1, keepdims=True)
            gc_last = jnp.sum(g, axis=0, keepdims=True)
            gamma[u] = jnp.where(causal, jnp.exp(jnp.where(causal, gc - gc_row, 0.0)), 0.0)
            egc = jnp.exp(gc)
            kb = k * beta
            q16[u], k16[u], kb16[u] = q.astype(bf16), k.astype(bf16), kb.astype(bf16)
            rhs16[u] = jnp.concatenate([kb * egc, v * beta], axis=-1).astype(bf16)
            q_dec[u] = q * egc
            k_dec[u] = k * jnp.exp(gc_last - gc)
            decay[u] = jnp.exp(gc_last)

    kk = {u: _dot_nt(kb16[u], k16[u]) for u in units}
    qk = {u: _dot_nt(q16[u], k16[u]) for u in units}
    a = {u: jnp.where(strict, kk[u] * gamma[u], 0.0) for u in units}
    attn = {u: qk[u] * gamma[u] for u in units}

    base = 8
    d = {u: jnp.where(ii // base == jj // base, a[u], 0.0) for u in units}
    t = {u: eye - d[u] for u in units}
    d16 = {u: d[u].astype(bf16) for u in units}
    pw16 = {u: _dot(d16[u], d16[u]).astype(bf16) for u in units}
    t = {u: t[u] + _dot(t[u].astype(bf16), pw16[u]) for u in units}
    pw16 = {u: _dot(pw16[u], pw16[u]).astype(bf16) for u in units}
    t = {u: t[u] + _dot(t[u].astype(bf16), pw16[u]) for u in units}
    size = base
    while size < CHUNK:
        off = (ii // (2 * size) == jj // (2 * size)) & (ii // size != jj // size)
        t16 = {u: t[u].astype(bf16) for u in units}
        ta = {u: _dot(t16[u], jnp.where(off, a[u], 0.0).astype(bf16)) for u in units}
        t = {u: t[u] - _dot(ta[u].astype(bf16), t16[u]) for u in units}
        size *= 2
    sol = {u: _dot(t[u].astype(bf16), rhs16[u]) for u in units}

    s_cur = {}
    for n in range(GDN_SLOTS):
        sq = n // per_seq
        if n % per_seq == 0:
            s_cur = {h: s_scr[sq, h] for h in range(H)}
        ws = {h: _dot(jnp.concatenate([sol[(n, h)][:, :HD], q_dec[(n, h)]], axis=0).astype(bf16), s_cur[h].astype(bf16))
              for h in range(H)}
        vn16 = {h: (sol[(n, h)][:, HD:] - ws[h][:CHUNK]).astype(bf16) for h in range(H)}
        r = {h: _dot(jnp.concatenate([attn[(n, h)], k_dec[(n, h)].T], axis=0).astype(bf16), vn16[h]) for h in range(H)}
        for h in range(H):
            cols = slice(h * HD, (h + 1) * HD)
            rows = slice(n * CHUNK, (n + 1) * CHUNK)
            s_cur[h] = s_cur[h] * decay[(n, h)] + r[h][CHUNK:]
            o = ws[h][CHUNK:] + r[h][:CHUNK]
            o = o * lax.rsqrt(jnp.mean(o * o, axis=-1, keepdims=True) + EPS) * gw_ref[...]
            o_ref[rows, cols] = (o * _silu(z_ref[rows, cols])).astype(bf16)
        if n % per_seq == per_seq - 1:
            for h in range(H):
                s_scr[sq, h] = s_cur[h]

    @pl.when(pl.program_id(1) == pl.num_programs(1) - 1)
    def _():
        sfin_ref[...] = s_scr[...]


def _gdn_call(p, hist, s0, conv_w, a_log, dt_bias, gdn_norm_w, layer, n_blocks, steps, n_seq):
    br = GDN_SLOTS * CHUNK
    rows = lambda col, width: pl.BlockSpec((br, width), lambda b, c: (b * steps + c, col))
    vec = pl.BlockSpec((None, 1, HD), lambda b, c: (layer, 0, 0))
    return pl.pallas_call(
        functools.partial(_gdn_kernel, n_seq=n_seq),
        grid=(n_blocks, steps),
        in_specs=[
            rows(COL_CONV // CONV_CH, CONV_CH),
            rows(COL_Z // D_GDN, D_GDN),
            rows(COL_AB // HD, HD),
            pl.BlockSpec((n_seq, 8, CONV_CH), lambda b, c: (b, 0, 0)),
            pl.BlockSpec((n_seq, H, HD, HD), lambda b, c: (b, 0, 0, 0)),
            pl.BlockSpec((None, CONV_W, CONV_CH), lambda b, c: (layer, 0, 0)),
            vec, vec, vec,
        ],
        out_specs=[
            pl.BlockSpec((br, D_GDN), lambda b, c: (b * steps + c, 0)),
            pl.BlockSpec((n_seq, H, HD, HD), lambda b, c: (b, 0, 0, 0)),
            pl.BlockSpec((n_seq, 8, CONV_CH), lambda b, c: (b, 0, 0)),
        ],
        out_shape=[
            jax.ShapeDtypeStruct((n_blocks * steps * br, D_GDN), bf16),
            jax.ShapeDtypeStruct((n_blocks * n_seq, H, HD, HD), f32),
            jax.ShapeDtypeStruct((n_blocks * n_seq, 8, CONV_CH), f32),
        ],
        scratch_shapes=[pltpu.VMEM((n_seq, 16 + br // n_seq, CONV_CH), f32), pltpu.VMEM((n_seq, H, HD, HD), f32)],
        compiler_params=_params(("arbitrary", "arbitrary"), 48),
        name="gated_deltanet",
    )(p, p, p, hist, s0, conv_w, a_log, dt_bias, gdn_norm_w)


def _pad_lanes(v):
    return jnp.pad(v, ((0, 0), (0, HD - v.shape[-1]))).reshape(DEPTH, 1, HD)


def kernel(x_prompt, x_sample, cache_k, cache_v, state_gdn, state_conv, c_prompt, c_sample, w_mod, b_mod, norm_g, ffn_up, ffn_down, w_in, w_out, qk_norm, lam_qk, subln_w, rpb_table, conv_w, a_log, dt_bias, gdn_norm_w):
    xp = x_prompt.reshape(P_TOK, D)
    xs = x_sample.reshape(S_TOK, D)
    c_all = jnp.concatenate([c_prompt, c_sample, jnp.zeros((N_CBATCH - BATCH - DEC_BATCH, D), f32)], axis=0)

    w_up16 = ffn_up.astype(bf16)
    w_down16 = ffn_down.astype(bf16)
    w_in16 = jnp.pad(w_in.astype(bf16), ((0, 0), (0, 0), (0, N_PROJ - N_IN)))
    w_out16 = w_out.astype(bf16)
    norm_g4 = norm_g.reshape(DEPTH, 3, 1, D)
    qk_norm2 = jnp.concatenate([qk_norm, qk_norm], axis=-1).reshape(DEPTH, 2, 1, HD)
    gdn_w = gdn_norm_w.reshape(DEPTH, 1, HD)
    a_log_p, dt_bias_p = _pad_lanes(a_log), _pad_lanes(dt_bias)
    ck = cache_k.reshape(DEPTH, DEC_BATCH, PAST * H, HD)
    cv = cache_v.reshape(DEPTH, DEC_BATCH, PAST * H, HD)
    hist_s = jnp.pad(state_conv, ((0, 0), (0, 0), (8 - (CONV_W - 1), 0), (0, 0)))
    hist_p = jnp.zeros((BATCH, 8, CONV_CH), f32)
    s0_p = jnp.zeros((BATCH, H, HD, HD), f32)

    mod = _mod_call(c_all, w_mod, b_mod).reshape(DEPTH, N_CBATCH, N_MOD, D)
    modg_p = jnp.transpose(jnp.repeat(mod[:, :BATCH], SEQ // GROUP, axis=1), (0, 2, 1, 3))
    modg_s = jnp.transpose(mod[:, BATCH:BATCH + DEC_BATCH], (0, 2, 1, 3))

    pbias, sbias = _bias_call(rpb_table)
    fast = _softmax_reference(qk_norm, rpb_table)

    gdn_rows = GDN_SLOTS * CHUNK
    outs = {n: [] for n in ("kp", "vp", "sp", "cp", "ks", "vs", "ss", "cs")}
    for l in range(DEPTH):
        lam_init = 0.8 - 0.6 * math.exp(-0.3 * l)
        xp = _ffn_call(xp, modg_p[l], norm_g4, w_up16, w_down16, l, 0, 0)
        xs = _ffn_call(xs, modg_s[l], norm_g4, w_up16, w_down16, l, 0, 0)
        pp = _proj_call(xp, modg_p[l], norm_g4, w_in16, l)
        ps = _proj_call(xs, modg_s[l], norm_g4, w_in16, l)

        oa_p, k_p, v_p = _attn_prompt_call(pp, pbias, fast, lam_qk, qk_norm2, subln_w, l, lam_init)
        oa_s, k_s, v_s = _attn_sample_call(ps, sbias, ck, cv, lam_qk, qk_norm2, subln_w, l, lam_init)
        og_p, s_p, c_p = _gdn_call(pp, hist_p, s0_p, conv_w, a_log_p, dt_bias_p, gdn_w, l, BATCH, SEQ // gdn_rows, 1)
        og_s, s_s, c_s = _gdn_call(ps, hist_s[l], state_gdn[l], conv_w, a_log_p, dt_bias_p, gdn_w, l,
                                   DEC_BATCH // GDN_SLOTS, 1, GDN_SLOTS)

        xp = _out_call(xp, modg_p[l], oa_p, og_p, w_out16, l)
        xs = _out_call(xs, modg_s[l], oa_s, og_s, w_out16, l)
        xp = _ffn_call(xp, modg_p[l], norm_g4, w_up16, w_down16, l, 1, 2)
        xs = _ffn_call(xs, modg_s[l], norm_g4, w_up16, w_down16, l, 1, 2)

        outs["kp"].append(k_p.reshape(BATCH, SEQ, H, HD))
        outs["vp"].append(v_p.reshape(BATCH, SEQ, H, HD))
        outs["sp"].append(s_p)
        outs["cp"].append(c_p[:, 8 - (CONV_W - 1):])
        outs["ks"].append(k_s.reshape(DEC_BATCH, DEC_SEQ, H, HD))
        outs["vs"].append(v_s.reshape(DEC_BATCH, DEC_SEQ, H, HD))
        outs["ss"].append(s_s)
        outs["cs"].append(c_s[:, 8 - (CONV_W - 1):])

    st = {n: jnp.stack(v) for n, v in outs.items()}
    return (xp.reshape(BATCH, SEQ, D), xs.reshape(DEC_BATCH, DEC_SEQ, D),
            st["kp"], st["vp"], st["sp"], st["cp"], st["ks"], st["vs"], st["ss"], st["cs"])
```

```python
import functools
import math

import jax
import jax.numpy as jnp
from jax import lax
from jax.experimental import pallas as pl
from jax.experimental.pallas import tpu as pltpu

f32 = jnp.float32
bf16 = jnp.bfloat16

D = 2048
BATCH = 2
SEQ = 4096
DEPTH = 2
DEC_BATCH = 32
DEC_SEQ = 64
PAST = 2048
CHUNK = 64
H = 8
HD = 128
HALF = 64
D_ATT = H * HD
D_GDN = H * HD
CONV_W = 4
CONV_CH = 3 * D_GDN
D_FF = 5632
NUM_BUCKETS = 32
FAR_BUCKET = 15
FAR_DIST = 91
N_MOD = 9
N_IN = 3 * D_ATT + CONV_CH + D_GDN + 2 * H
EPS = 1e-6

GROUP = 64
P_TOK = BATCH * SEQ
S_TOK = DEC_BATCH * DEC_SEQ
N_CBATCH = 40
N_PROJ = 7680
COL_K, COL_V, COL_CONV, COL_Z, COL_AB = D_ATT, 2 * D_ATT, 3 * D_ATT, 3 * D_ATT + CONV_CH, 3 * D_ATT + CONV_CH + D_GDN

MASKED = -1e30
SUB = 128

FFN_TM, FFN_FC = 512, 512
PROJ_TM, PROJ_TN = 1024, 512
OUT_TM = 512
MOD_TN = 1024
ATT_TQ = 512
ATT_TK = 512
ATT_NG = 4
ATT_LG = 2 * ATT_TQ // ATT_NG
LOG2E = math.log2(math.e)
MAX_SPAN = 100.0
SATT_TK = 512
GDN_SLOTS = 2


def _params(sem, vmem_mb):
    return pltpu.CompilerParams(dimension_semantics=sem, vmem_limit_bytes=vmem_mb * 1024 * 1024)


def _dot(a, b):
    return jnp.dot(a, b, preferred_element_type=f32)


def _dot_nt(a, b):
    return lax.dot_general(a, b, (((1,), (1,)), ((), ())), preferred_element_type=f32)


def _silu(x):
    return x * jax.nn.sigmoid(x)


def _modulated_norm(x, ng, scale, shift):
    tm = x.shape[0]
    ms = jnp.mean(x * x, axis=-1, keepdims=True)
    y = x * lax.rsqrt(ms + EPS) * ng
    y = y.reshape(tm // GROUP, GROUP, D) * (1.0 + scale[:, None, :]) + shift[:, None, :]
    return y.reshape(tm, D)


def _gated(y, gate):
    tm = y.shape[0]
    return (y.reshape(tm // GROUP, GROUP, D) * gate[:, None, :]).reshape(tm, D)


def _blockdiag_half():
    r = lax.broadcasted_iota(jnp.int32, (HD, HD), 0) // HALF
    c = lax.broadcasted_iota(jnp.int32, (HD, HD), 1) // HALF
    return (r == c).astype(bf16)


def _half_rms(x, w, bd):
    ss = _dot((x * x).astype(bf16), bd) * (1.0 / HALF)
    return x * lax.rsqrt(ss + EPS) * w


def _mod_kernel(c_ref, w_ref, b_ref, o_ref):
    a = _silu(c_ref[...]).astype(bf16)
    o_ref[...] = _dot(a, w_ref[...].astype(bf16)) + b_ref[...]


def _mod_call(c_all, w_mod, b_mod):
    n = N_MOD * D // MOD_TN
    return pl.pallas_call(
        _mod_kernel,
        grid=(DEPTH, n),
        in_specs=[
            pl.BlockSpec((N_CBATCH, D), lambda l, j: (0, 0)),
            pl.BlockSpec((None, D, MOD_TN), lambda l, j: (l, 0, j)),
            pl.BlockSpec((None, 1, MOD_TN), lambda l, j: (l, 0, j)),
        ],
        out_specs=pl.BlockSpec((None, N_CBATCH, MOD_TN), lambda l, j: (l, 0, j)),
        out_shape=jax.ShapeDtypeStruct((DEPTH, N_CBATCH, N_MOD * D), f32),
        compiler_params=_params(("arbitrary", "arbitrary"), 40),
        name="adaln_mod",
    )(c_all, w_mod, b_mod.reshape(DEPTH, 1, N_MOD * D))


def _ffn_kernel(x_ref, xn_ref, shift0_ref, scale0_ref, shiftn_ref, scalen_ref, gate_ref, ng_ref,
                wg_ref, wu_ref, wd_ref, o_ref, h_scr, acc_scr):
    m, k = pl.program_id(0), pl.program_id(1)
    slot = m % 2

    @pl.when((m == 0) & (k == 0))
    def _():
        h_scr[0] = _modulated_norm(x_ref[...], ng_ref[...], scale0_ref[...], shift0_ref[...]).astype(bf16)

    def ff_chunk():
        h = h_scr[slot]
        act = (_silu(_dot(h, wg_ref[...])) * _dot(h, wu_ref[...])).astype(bf16)
        acc_scr[...] = jnp.where(k == 0, 0.0, acc_scr[...]) + _dot(act, wd_ref[...])

    @pl.when(k < pl.num_programs(1) - 1)
    def _():
        ff_chunk()

    @pl.when(k == pl.num_programs(1) - 1)
    def _():
        ff_chunk()
        h_scr[1 - slot] = _modulated_norm(xn_ref[...], ng_ref[...], scalen_ref[...], shiftn_ref[...]).astype(bf16)
        o_ref[...] = x_ref[...] + 0.5 * _gated(acc_scr[...], gate_ref[...])


def _ffn_call(x, modg, norm_g, w_up, w_down, layer, idx, j):
    n_tok = x.shape[0]
    nm = n_tok // FFN_TM
    nk = D_FF // FFN_FC
    gm = FFN_TM // GROUP
    nxt = lambda m: jnp.minimum(m + 1, nm - 1)
    return pl.pallas_call(
        _ffn_kernel,
        grid=(nm, nk),
        in_specs=[
            pl.BlockSpec((FFN_TM, D), lambda m, k: (m, 0)),
            pl.BlockSpec((FFN_TM, D), lambda m, k: (nxt(m), 0)),
            pl.BlockSpec((None, gm, D), lambda m, k: (3 * j, 0, 0)),
            pl.BlockSpec((None, gm, D), lambda m, k: (3 * j + 1, 0, 0)),
            pl.BlockSpec((None, gm, D), lambda m, k: (3 * j, nxt(m), 0)),
            pl.BlockSpec((None, gm, D), lambda m, k: (3 * j + 1, nxt(m), 0)),
            pl.BlockSpec((None, gm, D), lambda m, k: (3 * j + 2, m, 0)),
            pl.BlockSpec((None, None, 1, D), lambda m, k: (layer, j, 0, 0)),
            pl.BlockSpec((None, None, D, FFN_FC), lambda m, k: (layer, idx, 0, k)),
            pl.BlockSpec((None, None, D, FFN_FC), lambda m, k: (layer, idx, 0, nk + k)),
            pl.BlockSpec((None, None, FFN_FC, D), lambda m, k: (layer, idx, k, 0)),
        ],
        out_specs=pl.BlockSpec((FFN_TM, D), lambda m, k: (m, 0)),
        out_shape=jax.ShapeDtypeStruct((n_tok, D), f32),
        scratch_shapes=[pltpu.VMEM((2, FFN_TM, D), bf16), pltpu.VMEM((FFN_TM, D), f32)],
        compiler_params=_params(("arbitrary", "arbitrary"), 56),
        name="swiglu_half_step",
    )(x, x, modg, modg, modg, modg, modg, norm_g, w_up, w_up, w_down)


def _proj_kernel(x_ref, shift_ref, scale_ref, ng_ref, w_ref, o_ref, h_scr):
    @pl.when(pl.program_id(1) == 0)
    def _():
        h_scr[...] = _modulated_norm(x_ref[...], ng_ref[...], scale_ref[...], shift_ref[...]).astype(bf16)

    o_ref[...] = _dot(h_scr[...], w_ref[...])


def _proj_call(x, modg, norm_g, w_in, layer):
    n_tok = x.shape[0]
    gm = PROJ_TM // GROUP
    row = lambda r: pl.BlockSpec((None, gm, D), lambda m, n: (r, m, 0))
    return pl.pallas_call(
        _proj_kernel,
        grid=(n_tok // PROJ_TM, N_PROJ // PROJ_TN),
        in_specs=[
            pl.BlockSpec((PROJ_TM, D), lambda m, n: (m, 0)),
            row(3), row(4),
            pl.BlockSpec((None, None, 1, D), lambda m, n: (layer, 1, 0, 0)),
            pl.BlockSpec((None, D, PROJ_TN), lambda m, n: (layer, 0, n)),
        ],
        out_specs=pl.BlockSpec((PROJ_TM, PROJ_TN), lambda m, n: (m, n)),
        out_shape=jax.ShapeDtypeStruct((n_tok, N_PROJ), f32),
        scratch_shapes=[pltpu.VMEM((PROJ_TM, D), bf16)],
        compiler_params=_params(("arbitrary", "arbitrary"), 40),
        name="mixer_in_proj",
    )(x, modg, modg, norm_g, w_in)


def _out_kernel(x_ref, gate_ref, a_ref, g_ref, w_ref, o_ref):
    y = _dot(a_ref[...], w_ref[:D_ATT, :]) + _dot(g_ref[...], w_ref[D_ATT:, :])
    o_ref[...] = x_ref[...] + _gated(y, gate_ref[...])


def _out_call(x, modg, o_att, o_gdn, w_out, layer):
    n_tok = x.shape[0]
    gm = OUT_TM // GROUP
    return pl.pallas_call(
        _out_kernel,
        grid=(n_tok // OUT_TM,),
        in_specs=[
            pl.BlockSpec((OUT_TM, D), lambda m: (m, 0)),
            pl.BlockSpec((None, gm, D), lambda m: (5, m, 0)),
            pl.BlockSpec((OUT_TM, D_ATT), lambda m: (m, 0)),
            pl.BlockSpec((OUT_TM, D_GDN), lambda m: (m, 0)),
            pl.BlockSpec((None, D, D), lambda m: (layer, 0, 0)),
        ],
        out_specs=pl.BlockSpec((OUT_TM, D), lambda m: (m, 0)),
        out_shape=jax.ShapeDtypeStruct((n_tok, D), f32),
        compiler_params=_params(("arbitrary",), 48),
        name="mixer_out_proj",
    )(x, modg, o_att, o_gdn, w_out)


def _bias_from_rel(rel, table_ref, h):
    n = jnp.abs(rel)
    big = jnp.full_like(n, 8)
    for t in (12, 16, 23, 32, 46, 64, FAR_DIST):
        big = big + (n >= t).astype(jnp.int32)
    bucket = jnp.where(n < 8, n, big) + jnp.where(rel > 0, NUM_BUCKETS // 2, 0)
    out = jnp.zeros(rel.shape, f32)
    for b in range(NUM_BUCKETS):
        out = jnp.where(bucket == b, table_ref[b, h], out)
    return out


def _fill_bias(dst_ref, lead, n_rows, n_cols, col0, key_axis, key_base, table_ref, h, scale=1.0):
    far = table_ref[FAR_BUCKET, h]
    for r0 in range(0, n_rows, SUB):
        nr = min(SUB, n_rows - r0)
        for c0 in range(0, n_cols, SUB):
            r = lax.broadcasted_iota(jnp.int32, (nr, SUB), 0) + r0
            c = lax.broadcasted_iota(jnp.int32, (nr, SUB), 1) + c0
            key, query = (r + key_base, c) if key_axis == 0 else (c + key_base, r)
            k_lo, q_lo = (r0 + key_base, c0) if key_axis == 0 else (c0 + key_base, r0)
            k_hi = k_lo + (nr if key_axis == 0 else SUB) - 1
            q_hi = q_lo + (SUB if key_axis == 0 else nr) - 1
            if k_lo // CHUNK > q_hi // CHUNK:
                tile = jnp.full((nr, SUB), MASKED, f32)
            elif k_hi - q_lo <= -FAR_DIST:
                tile = jnp.zeros((nr, SUB), f32)
            else:
                tile = (_bias_from_rel(key - query, table_ref, h) - far) * scale
                tile = jnp.where(key // CHUNK <= query // CHUNK, tile, MASKED)
            dst_ref[lead + (slice(r0, r0 + nr), slice(col0 + c0, col0 + c0 + SUB))] = tile


def _bias_kernel(table_ref, pb_ref, sb_ref):
    h = pl.program_id(0)
    _fill_bias(pb_ref, (0,), 2 * ATT_TQ, ATT_TQ, 0, 0, 0, table_ref, h, LOG2E)
    _fill_bias(pb_ref, (1,), 2 * ATT_TQ, ATT_TQ, 0, 0, -ATT_TQ, table_ref, h, LOG2E)
    _fill_bias(sb_ref, (), DEC_SEQ, SATT_TK, 0, 1, -SATT_TK, table_ref, h)
    _fill_bias(sb_ref, (), DEC_SEQ, HD, SATT_TK, 1, 0, table_ref, h)


def _bias_call(rpb_table):
    return pl.pallas_call(
        _bias_kernel,
        grid=(H,),
        in_specs=[pl.BlockSpec(memory_space=pltpu.SMEM)],
        out_specs=[
            pl.BlockSpec((None, 2, 2 * ATT_TQ, ATT_TQ), lambda h: (h, 0, 0, 0)),
            pl.BlockSpec((None, DEC_SEQ, SATT_TK + HD), lambda h: (h, 0, 0)),
        ],
        out_shape=[
            jax.ShapeDtypeStruct((H, 2, 2 * ATT_TQ, ATT_TQ), f32),
            jax.ShapeDtypeStruct((H, DEC_SEQ, SATT_TK + HD), f32),
        ],
        compiler_params=_params(("arbitrary",), 32),
        name="rel_pos_bias_tiles",
    )(rpb_table)


def _two_map_queries(qn):
    lane = lax.broadcasted_iota(jnp.int32, qn.shape, 1)
    return jnp.concatenate([jnp.where(lane < HALF, qn, 0.0), jnp.where(lane >= HALF, qn, 0.0)], axis=0).astype(bf16)


def _add_map_bias(s, b):
    n = b.shape[0]
    return (s.reshape(2, n, s.shape[1]) + b[None]).reshape(s.shape)


def _diff_combine(l, acc, lam_ref, sw, lam_init):
    n = acc.shape[0] // 2
    lq = lam_ref[...]
    lam = (jnp.exp(jnp.sum(lq[0:1] * lq[1:2], axis=-1, keepdims=True))
           - jnp.exp(jnp.sum(lq[2:3] * lq[3:4], axis=-1, keepdims=True)) + lam_init)
    o = acc[:n] / l[:n] - lam * (acc[n:] / l[n:])
    o = o * lax.rsqrt(jnp.mean(o * o, axis=-1, keepdims=True) + EPS) * sw
    return o * (1.0 - lam_init)


def _attn_prompt_kernel(fast_ref, lam_ref, qw_ref, kw_ref, sw_ref, bias_ref, q_ref, k_ref, v_ref, *rest,
                        layer, lam_init):
    o_ref, knew_ref, vnew_ref, kn_scr, vt_scr = rest[-5:]
    qi = pl.program_id(2)
    bd = _blockdiag_half()

    @pl.when(qi == 0)
    def _():
        if layer:
            knew_ref[:layer] = rest[0][...]
            vnew_ref[:layer] = rest[1][...]
        kn = _half_rms(k_ref[...], kw_ref[...], bd)
        knew_ref[layer] = kn
        kn_scr[...] = kn.astype(bf16)
        for c in range(SEQ // ATT_TK):
            v = v_ref[c * ATT_TK:(c + 1) * ATT_TK, :]
            vnew_ref[layer, c * ATT_TK:(c + 1) * ATT_TK, :] = v
            vt_scr[c] = v.T.astype(bf16)

    qt = (_half_rms(q_ref[...], qw_ref[...], bd) * (HALF ** -0.5 * LOG2E)).T
    feat = lax.broadcasted_iota(jnp.int32, qt.shape, 0)
    q2t = jnp.concatenate([jnp.where(feat < HALF, qt, 0.0), jnp.where(feat >= HALF, qt, 0.0)], axis=1).astype(bf16)
    q_groups = [q2t[:, g * ATT_LG:(g + 1) * ATT_LG] for g in range(ATT_NG)]

    def logits(j, groups):
        keys = kn_scr[pl.ds(pl.multiple_of(j * ATT_TK, ATT_TK), ATT_TK), :]
        return [_dot(keys, q_groups[g]) for g in groups]

    def attend(fixed_ref):
        def consume(j, g, s, state, bias):
            m, l, acc, p_prev = state
            acc = acc + _dot(vt_scr[jnp.maximum(j - 1, 0)], p_prev)
            if bias is not None:
                q0 = (g * ATT_LG) % ATT_TQ
                s = s + bias[:, q0:q0 + ATT_LG]
            if fixed_ref is not None:
                p = jnp.exp2(s - fixed_ref)
                return m, l + jnp.sum(p, axis=0, keepdims=True), acc, p.astype(bf16)
            m_new = jnp.maximum(m, jnp.max(s, axis=0, keepdims=True))
            p = jnp.exp2(s - m_new)
            alpha = jnp.exp2(m - m_new)
            return m_new, alpha * l + jnp.sum(p, axis=0, keepdims=True), alpha * acc, p.astype(bf16)

        def step(j, states, bias=None):
            s = logits(j, range(ATT_NG))
            return [consume(j, g, s[g], states[g], bias) for g in range(ATT_NG)]

        states = [(jnp.full((1, ATT_LG), MASKED, f32), jnp.zeros((1, ATT_LG), f32), jnp.zeros((HD, ATT_LG), f32),
                   jnp.zeros((ATT_TK, ATT_LG), bf16)) for _ in range(ATT_NG)]
        n_near = 2 * ATT_TQ // ATT_TK
        near0 = jnp.maximum(qi - 1, 0) * (ATT_TQ // ATT_TK)
        states = lax.fori_loop(0, near0, step, states)
        variant = jnp.minimum(qi, 1)
        for part in range(n_near):
            states = step(near0 + part, states, bias_ref[variant, part * ATT_TK:(part + 1) * ATT_TK, :])
        l = jnp.concatenate([st[1] for st in states], axis=1)
        acc = jnp.concatenate([st[2] + _dot(vt_scr[near0 + n_near - 1], st[3]) for st in states], axis=1)

        lq = lam_ref[...]
        lam = (jnp.exp(jnp.sum(lq[0:1] * lq[1:2], axis=-1, keepdims=True))
               - jnp.exp(jnp.sum(lq[2:3] * lq[3:4], axis=-1, keepdims=True)) + lam_init)
        o2 = acc / l
        o = (o2[:, :ATT_TQ] - lam * o2[:, ATT_TQ:]).T
        o = o * lax.rsqrt(jnp.mean(o * o, axis=-1, keepdims=True) + EPS) * sw_ref[...]
        o_ref[...] = (o * (1.0 - lam_init)).astype(bf16)

    row = layer * H + pl.program_id(1)
    no_underflow = fast_ref[row, 1] > 0.5

    @pl.when(no_underflow)
    def _():
        attend(fast_ref[row, 0])

    @pl.when(jnp.logical_not(no_underflow))
    def _():
        attend(None)


def _softmax_reference(qk_norm, rpb_table):
    wq = jnp.max(jnp.abs(qk_norm[:, 0]), axis=-1)
    wk = jnp.max(jnp.abs(qk_norm[:, 1]), axis=-1)
    bound = (HALF ** 0.5 * LOG2E * wq * wk)[:, None]
    shifted = (rpb_table - rpb_table[FAR_BUCKET]) * LOG2E
    hi, lo = jnp.max(shifted, axis=0)[None], jnp.min(shifted, axis=0)[None]
    flag = (2.0 * bound + hi - lo < MAX_SPAN).astype(f32)
    return jnp.stack([jnp.broadcast_to(bound + hi, flag.shape), flag], axis=-1).reshape(DEPTH * H, 2)


def _attn_prompt_call(p, bias, fast, lam_qk, qk_norm2, subln_w, layer, lam_init, kv_prev):
    nq = SEQ // ATT_TQ
    vec = lambda r: pl.BlockSpec((None, None, 1, HD), lambda b, h, i: (layer, r, 0, 0))
    head_seq = lambda n: pl.BlockSpec((n, SEQ, HD), lambda b, h, i: (0, b, h))
    stacked = jax.ShapeDtypeStruct((layer + 1, P_TOK, D_ATT), f32)
    return pl.pallas_call(
        functools.partial(_attn_prompt_kernel, layer=layer, lam_init=lam_init),
        grid=(BATCH, H, nq),
        in_specs=[
            pl.BlockSpec(memory_space=pltpu.SMEM),
            pl.BlockSpec((None, 4, HALF), lambda b, h, i: (layer, 0, 0)),
            vec(0), vec(1),
            pl.BlockSpec((None, 1, HD), lambda b, h, i: (layer, 0, 0)),
            pl.BlockSpec((None, 2, 2 * ATT_TQ, ATT_TQ), lambda b, h, i: (h, 0, 0, 0)),
            pl.BlockSpec((ATT_TQ, HD), lambda b, h, i: (b * nq + i, h)),
            pl.BlockSpec((SEQ, HD), lambda b, h, i: (b, H + h)),
            pl.BlockSpec((SEQ, HD), lambda b, h, i: (b, 2 * H + h)),
        ] + [head_seq(layer)] * len(kv_prev),
        out_specs=[pl.BlockSpec((ATT_TQ, HD), lambda b, h, i: (b * nq + i, h)), head_seq(layer + 1), head_seq(layer + 1)],
        out_shape=[jax.ShapeDtypeStruct((P_TOK, D_ATT), bf16), stacked, stacked],
        scratch_shapes=[pltpu.VMEM((SEQ, HD), bf16), pltpu.VMEM((SEQ // ATT_TK, HD, ATT_TK), bf16)],
        compiler_params=_params(("arbitrary", "arbitrary", "arbitrary"), 56),
        name="diff_attn_prompt",
    )(fast, lam_qk, qk_norm2, qk_norm2, subln_w.reshape(DEPTH, 1, HD), bias, p, p, p, *kv_prev)


def _attn_sample_kernel(lam_ref, qw_ref, kw_ref, sw_ref, bias_ref, q_ref, kx_ref, vx_ref, ck_ref, cv_ref, *rest,
                        layer, lam_init):
    o_ref, knew_ref, vnew_ref = rest[-3:]
    bd = _blockdiag_half()
    n_blk = PAST // SATT_TK
    heads, blocks = range(H), range(n_blk)
    cols = [slice(h * HD, (h + 1) * HD) for h in heads]
    if layer:
        knew_ref[:layer] = rest[0][...]
        vnew_ref[:layer] = rest[1][...]
    vnew_ref[layer] = vx_ref[...]
    q2 = [_two_map_queries(_half_rms(q_ref[:, cols[h]], qw_ref[...], bd) * (HALF ** -0.5)) for h in heads]
    kn = [_half_rms(kx_ref[:, cols[h]], kw_ref[...], bd) for h in heads]
    for h in heads:
        knew_ref[layer, :, cols[h]] = kn[h]

    def cached(ref, h, i):
        return ref[pl.ds(i * SATT_TK * H + h, SATT_TK, stride=H), :].astype(bf16)

    ss = [[_dot_nt(q2[h], cached(ck_ref, h, i)) for i in blocks] for h in heads]
    for h in heads:
        ss[h][-1] = _add_map_bias(ss[h][-1], bias_ref[h, :, :SATT_TK])
        ss[h].append(_add_map_bias(_dot_nt(q2[h], kn[h].astype(bf16)), bias_ref[h, :, SATT_TK:SATT_TK + DEC_SEQ]))
    for h in heads:
        m = functools.reduce(jnp.maximum, [jnp.max(s, axis=-1, keepdims=True) for s in ss[h]])
        ps = [jnp.exp(s - m) for s in ss[h]]
        l = functools.reduce(jnp.add, [jnp.sum(p, axis=-1, keepdims=True) for p in ps])
        vals = [cached(cv_ref, h, i) for i in blocks] + [vx_ref[:, cols[h]].astype(bf16)]
        acc = functools.reduce(jnp.add, [_dot(p.astype(bf16), v) for p, v in zip(ps, vals)])
        o_ref[:, cols[h]] = _diff_combine(l, acc, lam_ref, sw_ref[...], lam_init).astype(bf16)


def _attn_sample_call(p, bias, cache_k, cache_v, lam_qk, qk_norm2, subln_w, layer, lam_init, kv_prev):
    vec = lambda r: pl.BlockSpec((None, None, 1, HD), lambda b: (layer, r, 0, 0))
    sec = lambda c: pl.BlockSpec((DEC_SEQ, D_ATT), lambda b: (b, c))
    cache = pl.BlockSpec((None, None, PAST * H, HD), lambda b: (layer, b, 0, 0))
    stream = pl.BlockSpec((DEC_SEQ, D_ATT), lambda b: (b, 0))
    layers = lambda n: pl.BlockSpec((n, DEC_SEQ, D_ATT), lambda b: (0, b, 0))
    stacked = jax.ShapeDtypeStruct((layer + 1, S_TOK, D_ATT), f32)
    return pl.pallas_call(
        functools.partial(_attn_sample_kernel, layer=layer, lam_init=lam_init),
        grid=(DEC_BATCH,),
        in_specs=[
            pl.BlockSpec((None, 4, HALF), lambda b: (layer, 0, 0)),
            vec(0), vec(1),
            pl.BlockSpec((None, 1, HD), lambda b: (layer, 0, 0)),
            pl.BlockSpec((H, DEC_SEQ, SATT_TK + HD), lambda b: (0, 0, 0)),
            sec(0), sec(1), sec(2), cache, cache,
        ] + [layers(layer)] * len(kv_prev),
        out_specs=[stream, layers(layer + 1), layers(layer + 1)],
        out_shape=[jax.ShapeDtypeStruct((S_TOK, D_ATT), bf16), stacked, stacked],
        compiler_params=_params(("arbitrary",), 60),
        name="diff_attn_sample",
    )(lam_qk, qk_norm2, qk_norm2, subln_w.reshape(DEPTH, 1, HD), bias, p, p, p, cache_k, cache_v, *kv_prev)


def _gdn_kernel(p_ref, z_ref, ab_ref, hist_ref, s0_ref, cw_ref, alog_ref, dtb_ref, gw_ref,
                o_ref, sfin_ref, cnew_ref, xp_scr, s_scr, *, n_seq):
    per_seq = GDN_SLOTS // n_seq
    rows_seq = per_seq * CHUNK

    @pl.when(pl.program_id(1) == 0)
    def _():
        xp_scr[:, 0:8, :] = jnp.zeros((n_seq, 8, CONV_CH), f32)
        xp_scr[:, 8:16, :] = hist_ref[...]
        s_scr[...] = s0_ref[...]

    ys = []
    for sq in range(n_seq):
        xp_scr[sq, 16:16 + rows_seq, :] = p_ref[sq * rows_seq:(sq + 1) * rows_seq, :]
        x_cur = xp_scr[sq, 8:16 + rows_seq, :]
        x_prev = xp_scr[sq, 7:15 + rows_seq, :]
        pair_lo = x_cur * cw_ref[1:2, :] + x_prev * cw_ref[0:1, :]
        pair_hi = x_cur[8:] * cw_ref[3:4, :] + x_prev[8:] * cw_ref[2:3, :]
        y = _silu(pair_hi + pair_lo[6:6 + rows_seq])
        tail = xp_scr[sq, 8 + rows_seq:16 + rows_seq, :]
        xp_scr[sq, 8:16, :] = tail
        cnew_ref[sq] = tail
        ys += [y[c * CHUNK:(c + 1) * CHUNK] for c in range(per_seq)]

    ii = lax.broadcasted_iota(jnp.int32, (CHUNK, CHUNK), 0)
    jj = lax.broadcasted_iota(jnp.int32, (CHUNK, CHUNK), 1)
    causal = ii >= jj
    strict = ii > jj
    eye = jnp.where(ii == jj, 1.0, 0.0)
    ones = jnp.ones((HD, HD), bf16)

    units = [(n, h) for n in range(GDN_SLOTS) for h in range(H)]
    q16, k16, kb16, rhs16, gamma, q_dec, k_dec, decay = {}, {}, {}, {}, {}, {}, {}, {}
    for n in range(GDN_SLOTS):
        y = ys[n]
        ab = ab_ref[n * CHUNK:(n + 1) * CHUNK, :]
        pre = ab + dtb_ref[...]
        softplus = jnp.maximum(pre, 0.0) + jnp.log1p(jnp.exp(-jnp.abs(pre)))
        g_all = -jnp.exp(alog_ref[...]) * softplus
        beta_all = jax.nn.sigmoid(ab)
        for h in range(H):
            u = (n, h)
            qx = y[:, h * HD:(h + 1) * HD]
            kx = y[:, D_GDN + h * HD:D_GDN + (h + 1) * HD]
            v = y[:, 2 * D_GDN + h * HD:2 * D_GDN + (h + 1) * HD]
            q = qx * lax.rsqrt(_dot((qx * qx).astype(bf16), ones) + EPS) * (HD ** -0.5)
            k = kx * lax.rsqrt(_dot((kx * kx).astype(bf16), ones) + EPS)
            g = g_all[:, h:h + 1]
            beta = beta_all[:, H + h:H + h + 1]
            gc_row = jnp.sum(jnp.where(ii <= jj, g, 0.0), axis=0, keepdims=True)
            gc = jnp.sum(jnp.where(ii == jj, gc_row, 0.0), axis=1, keepdims=True)
            gc_last = jnp.sum(g, axis=0, keepdims=True)
            gamma[u] = jnp.where(causal, jnp.exp(jnp.where(causal, gc - gc_row, 0.0)), 0.0)
            egc = jnp.exp(gc)
            kb = k * beta
            q16[u], k16[u], kb16[u] = q.astype(bf16), k.astype(bf16), kb.astype(bf16)
            rhs16[u] = jnp.concatenate([kb * egc, v * beta], axis=-1).astype(bf16)
            q_dec[u] = q * egc
            k_dec[u] = k * jnp.exp(gc_last - gc)
            decay[u] = jnp.exp(gc_last)

    kk = {u: _dot_nt(kb16[u], k16[u]) for u in units}
    qk = {u: _dot_nt(q16[u], k16[u]) for u in units}
    a = {u: jnp.where(strict, kk[u] * gamma[u], 0.0) for u in units}
    attn = {u: qk[u] * gamma[u] for u in units}

    base = 8
    d = {u: jnp.where(ii // base == jj // base, a[u], 0.0) for u in units}
    t = {u: eye - d[u] for u in units}
    d16 = {u: d[u].astype(bf16) for u in units}
    pw16 = {u: _dot(d16[u], d16[u]).astype(bf16) for u in units}
    t = {u: t[u] + _dot(t[u].astype(bf16), pw16[u]) for u in units}
    pw16 = {u: _dot(pw16[u], pw16[u]).astype(bf16) for u in units}
    t = {u: t[u] + _dot(t[u].astype(bf16), pw16[u]) for u in units}
    size = base
    while size < CHUNK:
        off = (ii // (2 * size) == jj // (2 * size)) & (ii // size != jj // size)
        t16 = {u: t[u].astype(bf16) for u in units}
        ta = {u: _dot(t16[u], jnp.where(off, a[u], 0.0).astype(bf16)) for u in units}
        t = {u: t[u] - _dot(ta[u].astype(bf16), t16[u]) for u in units}
        size *= 2
    sol = {u: _dot(t[u].astype(bf16), rhs16[u]) for u in units}

    s_cur = {}
    for n in range(GDN_SLOTS):
        sq = n // per_seq
        if n % per_seq == 0:
            s_cur = {h: s_scr[sq, h] for h in range(H)}
        ws = {h: _dot(jnp.concatenate([sol[(n, h)][:, :HD], q_dec[(n, h)]], axis=0).astype(bf16), s_cur[h].astype(bf16))
              for h in range(H)}
        vn16 = {h: (sol[(n, h)][:, HD:] - ws[h][:CHUNK]).astype(bf16) for h in range(H)}
        r = {h: _dot(jnp.concatenate([attn[(n, h)], k_dec[(n, h)].T], axis=0).astype(bf16), vn16[h]) for h in range(H)}
        for h in range(H):
            cols = slice(h * HD, (h + 1) * HD)
            rows = slice(n * CHUNK, (n + 1) * CHUNK)
            s_cur[h] = s_cur[h] * decay[(n, h)] + r[h][CHUNK:]
            o = ws[h][CHUNK:] + r[h][:CHUNK]
            o = o * lax.rsqrt(jnp.mean(o * o, axis=-1, keepdims=True) + EPS) * gw_ref[...]
            o_ref[rows, cols] = (o * _silu(z_ref[rows, cols])).astype(bf16)
        if n % per_seq == per_seq - 1:
            for h in range(H):
                s_scr[sq, h] = s_cur[h]

    @pl.when(pl.program_id(1) == pl.num_programs(1) - 1)
    def _():
        sfin_ref[...] = s_scr[...]


def _gdn_call(p, hist, s0, conv_w, a_log, dt_bias, gdn_norm_w, layer, n_blocks, steps, n_seq):
    br = GDN_SLOTS * CHUNK
    rows = lambda col, width: pl.BlockSpec((br, width), lambda b, c: (b * steps + c, col))
    vec = pl.BlockSpec((None, 1, HD), lambda b, c: (layer, 0, 0))
    return pl.pallas_call(
        functools.partial(_gdn_kernel, n_seq=n_seq),
        grid=(n_blocks, steps),
        in_specs=[
            rows(COL_CONV // CONV_CH, CONV_CH),
            rows(COL_Z // D_GDN, D_GDN),
            rows(COL_AB // HD, HD),
            pl.BlockSpec((n_seq, 8, CONV_CH), lambda b, c: (b, 0, 0)),
            pl.BlockSpec((n_seq, H, HD, HD), lambda b, c: (b, 0, 0, 0)),
            pl.BlockSpec((None, CONV_W, CONV_CH), lambda b, c: (layer, 0, 0)),
            vec, vec, vec,
        ],
        out_specs=[
            pl.BlockSpec((br, D_GDN), lambda b, c: (b * steps + c, 0)),
            pl.BlockSpec((n_seq, H, HD, HD), lambda b, c: (b, 0, 0, 0)),
            pl.BlockSpec((n_seq, 8, CONV_CH), lambda b, c: (b, 0, 0)),
        ],
        out_shape=[
            jax.ShapeDtypeStruct((n_blocks * steps * br, D_GDN), bf16),
            jax.ShapeDtypeStruct((n_blocks * n_seq, H, HD, HD), f32),
            jax.ShapeDtypeStruct((n_blocks * n_seq, 8, CONV_CH), f32),
        ],
        scratch_shapes=[pltpu.VMEM((n_seq, 16 + br // n_seq, CONV_CH), f32), pltpu.VMEM((n_seq, H, HD, HD), f32)],
        compiler_params=_params(("arbitrary", "arbitrary"), 48),
        name="gated_deltanet",
    )(p, p, p, hist, s0, conv_w, a_log, dt_bias, gdn_norm_w)


def _pad_lanes(v):
    return jnp.pad(v, ((0, 0), (0, HD - v.shape[-1]))).reshape(DEPTH, 1, HD)


def kernel(x_prompt, x_sample, cache_k, cache_v, state_gdn, state_conv, c_prompt, c_sample, w_mod, b_mod, norm_g, ffn_up, ffn_down, w_in, w_out, qk_norm, lam_qk, subln_w, rpb_table, conv_w, a_log, dt_bias, gdn_norm_w):
    xp = x_prompt.reshape(P_TOK, D)
    xs = x_sample.reshape(S_TOK, D)
    c_all = jnp.concatenate([c_prompt, c_sample, jnp.zeros((N_CBATCH - BATCH - DEC_BATCH, D), f32)], axis=0)

    w_up16 = ffn_up.astype(bf16)
    w_down16 = ffn_down.astype(bf16)
    w_in16 = jnp.pad(w_in.astype(bf16), ((0, 0), (0, 0), (0, N_PROJ - N_IN)))
    w_out16 = w_out.astype(bf16)
    norm_g4 = norm_g.reshape(DEPTH, 3, 1, D)
    qk_norm2 = jnp.concatenate([qk_norm, qk_norm], axis=-1).reshape(DEPTH, 2, 1, HD)
    gdn_w = gdn_norm_w.reshape(DEPTH, 1, HD)
    a_log_p, dt_bias_p = _pad_lanes(a_log), _pad_lanes(dt_bias)
    ck = cache_k.reshape(DEPTH, DEC_BATCH, PAST * H, HD)
    cv = cache_v.reshape(DEPTH, DEC_BATCH, PAST * H, HD)
    hist_s = jnp.pad(state_conv, ((0, 0), (0, 0), (8 - (CONV_W - 1), 0), (0, 0)))
    hist_p = jnp.zeros((BATCH, 8, CONV_CH), f32)
    s0_p = jnp.zeros((BATCH, H, HD, HD), f32)

    mod = _mod_call(c_all, w_mod, b_mod).reshape(DEPTH, N_CBATCH, N_MOD, D)
    modg_p = jnp.transpose(jnp.repeat(mod[:, :BATCH], SEQ // GROUP, axis=1), (0, 2, 1, 3))
    modg_s = jnp.transpose(mod[:, BATCH:BATCH + DEC_BATCH], (0, 2, 1, 3))

    pbias, sbias = _bias_call(rpb_table)
    fast = _softmax_reference(qk_norm, rpb_table)

    gdn_rows = GDN_SLOTS * CHUNK
    outs = {n: [] for n in ("sp", "cp", "ss", "cs")}
    kv_p, kv_s = (), ()
    for l in range(DEPTH):
        lam_init = 0.8 - 0.6 * math.exp(-0.3 * l)
        xp = _ffn_call(xp, modg_p[l], norm_g4, w_up16, w_down16, l, 0, 0)
        xs = _ffn_call(xs, modg_s[l], norm_g4, w_up16, w_down16, l, 0, 0)
        pp = _proj_call(xp, modg_p[l], norm_g4, w_in16, l)
        ps = _proj_call(xs, modg_s[l], norm_g4, w_in16, l)

        oa_p, *kv_p = _attn_prompt_call(pp, pbias, fast, lam_qk, qk_norm2, subln_w, l, lam_init, kv_p)
        oa_s, *kv_s = _attn_sample_call(ps, sbias, ck, cv, lam_qk, qk_norm2, subln_w, l, lam_init, kv_s)
        og_p, s_p, c_p = _gdn_call(pp, hist_p, s0_p, conv_w, a_log_p, dt_bias_p, gdn_w, l, BATCH, SEQ // gdn_rows, 1)
        og_s, s_s, c_s = _gdn_call(ps, hist_s[l], state_gdn[l], conv_w, a_log_p, dt_bias_p, gdn_w, l,
                                   DEC_BATCH // GDN_SLOTS, 1, GDN_SLOTS)

        xp = _out_call(xp, modg_p[l], oa_p, og_p, w_out16, l)
        xs = _out_call(xs, modg_s[l], oa_s, og_s, w_out16, l)
        xp = _ffn_call(xp, modg_p[l], norm_g4, w_up16, w_down16, l, 1, 2)
        xs = _ffn_call(xs, modg_s[l], norm_g4, w_up16, w_down16, l, 1, 2)

        outs["sp"].append(s_p)
        outs["cp"].append(c_p[:, 8 - (CONV_W - 1):])
        outs["ss"].append(s_s)
        outs["cs"].append(c_s[:, 8 - (CONV_W - 1):])

    st = {n: jnp.stack(v) for n, v in outs.items()}
    return (xp.reshape(BATCH, SEQ, D), xs.reshape(DEC_BATCH, DEC_SEQ, D),
            kv_p[0].reshape(DEPTH, BATCH, SEQ, H, HD), kv_p[1].reshape(DEPTH, BATCH, SEQ, H, HD), st["sp"], st["cp"],
            kv_s[0].reshape(DEPTH, DEC_BATCH, DEC_SEQ, H, HD), kv_s[1].reshape(DEPTH, DEC_BATCH, DEC_SEQ, H, HD),
            st["ss"], st["cs"])
```

```python
import functools
import math

import jax
import jax.numpy as jnp
from jax import lax
from jax.experimental import pallas as pl
from jax.experimental.pallas import tpu as pltpu

f32 = jnp.float32
bf16 = jnp.bfloat16

D = 2048
BATCH = 2
SEQ = 4096
DEPTH = 2
DEC_BATCH = 32
DEC_SEQ = 64
PAST = 2048
CHUNK = 64
H = 8
HD = 128
HALF = 64
D_ATT = H * HD
D_GDN = H * HD
CONV_W = 4
CONV_CH = 3 * D_GDN
D_FF = 5632
NUM_BUCKETS = 32
FAR_BUCKET = 15
FAR_DIST = 91
N_MOD = 9
N_IN = 3 * D_ATT + CONV_CH + D_GDN + 2 * H
EPS = 1e-6

GROUP = 64
P_TOK = BATCH * SEQ
S_TOK = DEC_BATCH * DEC_SEQ
N_CBATCH = 40
N_PROJ = 7680
COL_K, COL_V, COL_CONV, COL_Z, COL_AB = D_ATT, 2 * D_ATT, 3 * D_ATT, 3 * D_ATT + CONV_CH, 3 * D_ATT + CONV_CH + D_GDN

MASKED = -1e30
SUB = 128

FFN_TM, FFN_FC = 512, 512
PROJ_TM, PROJ_TN = 1024, 512
OUT_TM = 512
MOD_TN = 2048
ATT_TQ = 512
ATT_TK = 512
ATT_NG = 4
ATT_LG = 2 * ATT_TQ // ATT_NG
LOG2E = math.log2(math.e)
MAX_SPAN = 100.0
SATT_TK = 512
GDN_SLOTS = 2


def _params(sem, vmem_mb):
    return pltpu.CompilerParams(dimension_semantics=sem, vmem_limit_bytes=vmem_mb * 1024 * 1024)


def _dot(a, b):
    return jnp.dot(a, b, preferred_element_type=f32)


def _dot_nt(a, b):
    return lax.dot_general(a, b, (((1,), (1,)), ((), ())), preferred_element_type=f32)


def _silu(x):
    return x * jax.nn.sigmoid(x)


def _modulated_norm(x, ng, scale, shift):
    tm = x.shape[0]
    ms = jnp.mean(x * x, axis=-1, keepdims=True)
    y = x * lax.rsqrt(ms + EPS) * ng
    y = y.reshape(tm // GROUP, GROUP, D) * (1.0 + scale[:, None, :]) + shift[:, None, :]
    return y.reshape(tm, D)


def _gated(y, gate):
    tm = y.shape[0]
    return (y.reshape(tm // GROUP, GROUP, D) * gate[:, None, :]).reshape(tm, D)


def _blockdiag_half():
    r = lax.broadcasted_iota(jnp.int32, (HD, HD), 0) // HALF
    c = lax.broadcasted_iota(jnp.int32, (HD, HD), 1) // HALF
    return (r == c).astype(bf16)


def _half_rms(x, w, bd):
    ss = _dot((x * x).astype(bf16), bd) * (1.0 / HALF)
    return x * lax.rsqrt(ss + EPS) * w


def _mod_kernel(c_ref, w_ref, b_ref, o_ref):
    a = _silu(c_ref[...]).astype(bf16)
    o_ref[...] = _dot(a, w_ref[...].astype(bf16)) + b_ref[...]


def _mod_call(c_all, w_mod, b_mod):
    n = N_MOD * D // MOD_TN
    return pl.pallas_call(
        _mod_kernel,
        grid=(DEPTH, n),
        in_specs=[
            pl.BlockSpec((N_CBATCH, D), lambda l, j: (0, 0)),
            pl.BlockSpec((None, D, MOD_TN), lambda l, j: (l, 0, j)),
            pl.BlockSpec((None, 1, MOD_TN), lambda l, j: (l, 0, j)),
        ],
        out_specs=pl.BlockSpec((None, N_CBATCH, MOD_TN), lambda l, j: (l, 0, j)),
        out_shape=jax.ShapeDtypeStruct((DEPTH, N_CBATCH, N_MOD * D), f32),
        compiler_params=_params(("arbitrary", "arbitrary"), 48),
        name="adaln_mod",
    )(c_all, w_mod, b_mod.reshape(DEPTH, 1, N_MOD * D))


def _ffn_kernel(x_ref, xn_ref, shift0_ref, scale0_ref, shiftn_ref, scalen_ref, gate_ref, ng_ref,
                wg_ref, wu_ref, wd_ref, o_ref, h_scr, acc_scr):
    m, k = pl.program_id(0), pl.program_id(1)
    slot = m % 2

    @pl.when((m == 0) & (k == 0))
    def _():
        h_scr[0] = _modulated_norm(x_ref[...], ng_ref[...], scale0_ref[...], shift0_ref[...]).astype(bf16)

    def ff_chunk():
        h = h_scr[slot]
        act = (_silu(_dot(h, wg_ref[...])) * _dot(h, wu_ref[...])).astype(bf16)
        acc_scr[...] = jnp.where(k == 0, 0.0, acc_scr[...]) + _dot(act, wd_ref[...])

    @pl.when(k < pl.num_programs(1) - 1)
    def _():
        ff_chunk()

    @pl.when(k == pl.num_programs(1) - 1)
    def _():
        ff_chunk()
        h_scr[1 - slot] = _modulated_norm(xn_ref[...], ng_ref[...], scalen_ref[...], shiftn_ref[...]).astype(bf16)
        o_ref[...] = x_ref[...] + 0.5 * _gated(acc_scr[...], gate_ref[...])


def _ffn_call(x, modg, norm_g, w_up, w_down, layer, idx, j):
    n_tok = x.shape[0]
    nm = n_tok // FFN_TM
    nk = D_FF // FFN_FC
    gm = FFN_TM // GROUP
    nxt = lambda m: jnp.minimum(m + 1, nm - 1)
    return pl.pallas_call(
        _ffn_kernel,
        grid=(nm, nk),
        in_specs=[
            pl.BlockSpec((FFN_TM, D), lambda m, k: (m, 0)),
            pl.BlockSpec((FFN_TM, D), lambda m, k: (nxt(m), 0)),
            pl.BlockSpec((None, gm, D), lambda m, k: (3 * j, 0, 0)),
            pl.BlockSpec((None, gm, D), lambda m, k: (3 * j + 1, 0, 0)),
            pl.BlockSpec((None, gm, D), lambda m, k: (3 * j, nxt(m), 0)),
            pl.BlockSpec((None, gm, D), lambda m, k: (3 * j + 1, nxt(m), 0)),
            pl.BlockSpec((None, gm, D), lambda m, k: (3 * j + 2, m, 0)),
            pl.BlockSpec((None, None, 1, D), lambda m, k: (layer, j, 0, 0)),
            pl.BlockSpec((None, None, D, FFN_FC), lambda m, k: (layer, idx, 0, k)),
            pl.BlockSpec((None, None, D, FFN_FC), lambda m, k: (layer, idx, 0, nk + k)),
            pl.BlockSpec((None, None, FFN_FC, D), lambda m, k: (layer, idx, k, 0)),
        ],
        out_specs=pl.BlockSpec((FFN_TM, D), lambda m, k: (m, 0)),
        out_shape=jax.ShapeDtypeStruct((n_tok, D), f32),
        scratch_shapes=[pltpu.VMEM((2, FFN_TM, D), bf16), pltpu.VMEM((FFN_TM, D), f32)],
        compiler_params=_params(("arbitrary", "arbitrary"), 56),
        name="swiglu_half_step",
    )(x, x, modg, modg, modg, modg, modg, norm_g, w_up, w_up, w_down)


def _proj_kernel(x_ref, shift_ref, scale_ref, ng_ref, w_ref, o_ref, h_scr):
    @pl.when(pl.program_id(1) == 0)
    def _():
        h_scr[...] = _modulated_norm(x_ref[...], ng_ref[...], scale_ref[...], shift_ref[...]).astype(bf16)

    o_ref[...] = _dot(h_scr[...], w_ref[...])


def _proj_call(x, modg, norm_g, w_in, layer):
    n_tok = x.shape[0]
    gm = PROJ_TM // GROUP
    row = lambda r: pl.BlockSpec((None, gm, D), lambda m, n: (r, m, 0))
    return pl.pallas_call(
        _proj_kernel,
        grid=(n_tok // PROJ_TM, N_PROJ // PROJ_TN),
        in_specs=[
            pl.BlockSpec((PROJ_TM, D), lambda m, n: (m, 0)),
            row(3), row(4),
            pl.BlockSpec((None, None, 1, D), lambda m, n: (layer, 1, 0, 0)),
            pl.BlockSpec((None, D, PROJ_TN), lambda m, n: (layer, 0, n)),
        ],
        out_specs=pl.BlockSpec((PROJ_TM, PROJ_TN), lambda m, n: (m, n)),
        out_shape=jax.ShapeDtypeStruct((n_tok, N_PROJ), f32),
        scratch_shapes=[pltpu.VMEM((PROJ_TM, D), bf16)],
        compiler_params=_params(("arbitrary", "arbitrary"), 40),
        name="mixer_in_proj",
    )(x, modg, modg, norm_g, w_in)


def _out_kernel(x_ref, gate_ref, a_ref, g_ref, w_ref, o_ref):
    y = _dot(a_ref[...], w_ref[:D_ATT, :]) + _dot(g_ref[...], w_ref[D_ATT:, :])
    o_ref[...] = x_ref[...] + _gated(y, gate_ref[...])


def _out_call(x, modg, o_att, o_gdn, w_out, layer):
    n_tok = x.shape[0]
    gm = OUT_TM // GROUP
    return pl.pallas_call(
        _out_kernel,
        grid=(n_tok // OUT_TM,),
        in_specs=[
            pl.BlockSpec((OUT_TM, D), lambda m: (m, 0)),
            pl.BlockSpec((None, gm, D), lambda m: (5, m, 0)),
            pl.BlockSpec((OUT_TM, D_ATT), lambda m: (m, 0)),
            pl.BlockSpec((OUT_TM, D_GDN), lambda m: (m, 0)),
            pl.BlockSpec((None, D, D), lambda m: (layer, 0, 0)),
        ],
        out_specs=pl.BlockSpec((OUT_TM, D), lambda m: (m, 0)),
        out_shape=jax.ShapeDtypeStruct((n_tok, D), f32),
        compiler_params=_params(("arbitrary",), 48),
        name="mixer_out_proj",
    )(x, modg, o_att, o_gdn, w_out)


def _bias_from_rel(rel, table_ref, h):
    n = jnp.abs(rel)
    big = jnp.full_like(n, 8)
    for t in (12, 16, 23, 32, 46, 64, FAR_DIST):
        big = big + (n >= t).astype(jnp.int32)
    bucket = jnp.where(n < 8, n, big) + jnp.where(rel > 0, NUM_BUCKETS // 2, 0)
    out = jnp.zeros(rel.shape, f32)
    for b in range(NUM_BUCKETS):
        out = jnp.where(bucket == b, table_ref[b, h], out)
    return out


def _fill_bias(dst_ref, lead, n_rows, n_cols, col0, key_axis, key_base, table_ref, h, scale=1.0):
    far = table_ref[FAR_BUCKET, h]
    for r0 in range(0, n_rows, SUB):
        nr = min(SUB, n_rows - r0)
        for c0 in range(0, n_cols, SUB):
            r = lax.broadcasted_iota(jnp.int32, (nr, SUB), 0) + r0
            c = lax.broadcasted_iota(jnp.int32, (nr, SUB), 1) + c0
            key, query = (r + key_base, c) if key_axis == 0 else (c + key_base, r)
            k_lo, q_lo = (r0 + key_base, c0) if key_axis == 0 else (c0 + key_base, r0)
            k_hi = k_lo + (nr if key_axis == 0 else SUB) - 1
            q_hi = q_lo + (SUB if key_axis == 0 else nr) - 1
            if k_lo // CHUNK > q_hi // CHUNK:
                tile = jnp.full((nr, SUB), MASKED, f32)
            elif k_hi - q_lo <= -FAR_DIST:
                tile = jnp.zeros((nr, SUB), f32)
            else:
                tile = (_bias_from_rel(key - query, table_ref, h) - far) * scale
                tile = jnp.where(key // CHUNK <= query // CHUNK, tile, MASKED)
            dst_ref[lead + (slice(r0, r0 + nr), slice(col0 + c0, col0 + c0 + SUB))] = tile


def _bias_kernel(table_ref, pb_ref, sb_ref):
    h = pl.program_id(0)
    _fill_bias(pb_ref, (0,), 2 * ATT_TQ, ATT_TQ, 0, 0, 0, table_ref, h, LOG2E)
    _fill_bias(pb_ref, (1,), 2 * ATT_TQ, ATT_TQ, 0, 0, -ATT_TQ, table_ref, h, LOG2E)
    _fill_bias(sb_ref, (), DEC_SEQ, SATT_TK, 0, 1, -SATT_TK, table_ref, h)
    _fill_bias(sb_ref, (), DEC_SEQ, HD, SATT_TK, 1, 0, table_ref, h)


def _bias_call(rpb_table):
    return pl.pallas_call(
        _bias_kernel,
        grid=(H,),
        in_specs=[pl.BlockSpec(memory_space=pltpu.SMEM)],
        out_specs=[
            pl.BlockSpec((None, 2, 2 * ATT_TQ, ATT_TQ), lambda h: (h, 0, 0, 0)),
            pl.BlockSpec((None, DEC_SEQ, SATT_TK + HD), lambda h: (h, 0, 0)),
        ],
        out_shape=[
            jax.ShapeDtypeStruct((H, 2, 2 * ATT_TQ, ATT_TQ), f32),
            jax.ShapeDtypeStruct((H, DEC_SEQ, SATT_TK + HD), f32),
        ],
        compiler_params=_params(("arbitrary",), 32),
        name="rel_pos_bias_tiles",
    )(rpb_table)


def _two_map_queries(qn):
    lane = lax.broadcasted_iota(jnp.int32, qn.shape, 1)
    return jnp.concatenate([jnp.where(lane < HALF, qn, 0.0), jnp.where(lane >= HALF, qn, 0.0)], axis=0).astype(bf16)


def _add_map_bias(s, b):
    n = b.shape[0]
    return (s.reshape(2, n, s.shape[1]) + b[None]).reshape(s.shape)


def _diff_combine(l, acc, lam_ref, sw, lam_init):
    n = acc.shape[0] // 2
    lq = lam_ref[...]
    lam = (jnp.exp(jnp.sum(lq[0:1] * lq[1:2], axis=-1, keepdims=True))
           - jnp.exp(jnp.sum(lq[2:3] * lq[3:4], axis=-1, keepdims=True)) + lam_init)
    o = acc[:n] / l[:n] - lam * (acc[n:] / l[n:])
    o = o * lax.rsqrt(jnp.mean(o * o, axis=-1, keepdims=True) + EPS) * sw
    return o * (1.0 - lam_init)


def _attn_prompt_kernel(fast_ref, lam_ref, qw_ref, kw_ref, sw_ref, bias_ref, q_ref, k_ref, v_ref, *rest,
                        layer, lam_init):
    o_ref, knew_ref, vnew_ref, kn_scr, vt_scr = rest[-5:]
    qi = pl.program_id(2)
    bd = _blockdiag_half()

    @pl.when(qi == 0)
    def _():
        if layer:
            knew_ref[:layer] = rest[0][...]
            vnew_ref[:layer] = rest[1][...]
        kn = _half_rms(k_ref[...], kw_ref[...], bd)
        knew_ref[layer] = kn
        kn_scr[...] = kn.astype(bf16)
        for c in range(SEQ // ATT_TK):
            v = v_ref[c * ATT_TK:(c + 1) * ATT_TK, :]
            vnew_ref[layer, c * ATT_TK:(c + 1) * ATT_TK, :] = v
            vt_scr[c] = v.T.astype(bf16)

    qt = (_half_rms(q_ref[...], qw_ref[...], bd) * (HALF ** -0.5 * LOG2E)).T
    feat = lax.broadcasted_iota(jnp.int32, qt.shape, 0)
    q2t = jnp.concatenate([jnp.where(feat < HALF, qt, 0.0), jnp.where(feat >= HALF, qt, 0.0)], axis=1).astype(bf16)
    q_groups = [q2t[:, g * ATT_LG:(g + 1) * ATT_LG] for g in range(ATT_NG)]

    def logits(j, groups):
        keys = kn_scr[pl.ds(pl.multiple_of(j * ATT_TK, ATT_TK), ATT_TK), :]
        return [_dot(keys, q_groups[g]) for g in groups]

    def attend(fixed_ref):
        def consume(j, g, s, state, bias):
            m, l, acc, p_prev = state
            acc = acc + _dot(vt_scr[jnp.maximum(j - 1, 0)], p_prev)
            if bias is not None:
                q0 = (g * ATT_LG) % ATT_TQ
                s = s + bias[:, q0:q0 + ATT_LG]
            if fixed_ref is not None:
                p = jnp.exp2(s - fixed_ref)
                return m, l + jnp.sum(p, axis=0, keepdims=True), acc, p.astype(bf16)
            m_new = jnp.maximum(m, jnp.max(s, axis=0, keepdims=True))
            p = jnp.exp2(s - m_new)
            alpha = jnp.exp2(m - m_new)
            return m_new, alpha * l + jnp.sum(p, axis=0, keepdims=True), alpha * acc, p.astype(bf16)

        def step(j, states, bias=None):
            s = logits(j, range(ATT_NG))
            return [consume(j, g, s[g], states[g], bias) for g in range(ATT_NG)]

        states = [(jnp.full((1, ATT_LG), MASKED, f32), jnp.zeros((1, ATT_LG), f32), jnp.zeros((HD, ATT_LG), f32),
                   jnp.zeros((ATT_TK, ATT_LG), bf16)) for _ in range(ATT_NG)]
        n_near = 2 * ATT_TQ // ATT_TK
        near0 = jnp.maximum(qi - 1, 0) * (ATT_TQ // ATT_TK)
        states = lax.fori_loop(0, near0, step, states)
        variant = jnp.minimum(qi, 1)
        for part in range(n_near):
            states = step(near0 + part, states, bias_ref[variant, part * ATT_TK:(part + 1) * ATT_TK, :])
        l = jnp.concatenate([st[1] for st in states], axis=1)
        acc = jnp.concatenate([st[2] + _dot(vt_scr[near0 + n_near - 1], st[3]) for st in states], axis=1)

        lq = lam_ref[...]
        lam = (jnp.exp(jnp.sum(lq[0:1] * lq[1:2], axis=-1, keepdims=True))
               - jnp.exp(jnp.sum(lq[2:3] * lq[3:4], axis=-1, keepdims=True)) + lam_init)
        o2 = acc / l
        o = (o2[:, :ATT_TQ] - lam * o2[:, ATT_TQ:]).T
        o = o * lax.rsqrt(jnp.mean(o * o, axis=-1, keepdims=True) + EPS) * sw_ref[...]
        o_ref[...] = (o * (1.0 - lam_init)).astype(bf16)

    row = layer * H + pl.program_id(1)
    no_underflow = fast_ref[row, 1] > 0.5

    @pl.when(no_underflow)
    def _():
        attend(fast_ref[row, 0])

    @pl.when(jnp.logical_not(no_underflow))
    def _():
        attend(None)


def _softmax_reference(qk_norm, rpb_table):
    wq = jnp.max(jnp.abs(qk_norm[:, 0]), axis=-1)
    wk = jnp.max(jnp.abs(qk_norm[:, 1]), axis=-1)
    bound = (HALF ** 0.5 * LOG2E * wq * wk)[:, None]
    shifted = (rpb_table - rpb_table[FAR_BUCKET]) * LOG2E
    hi, lo = jnp.max(shifted, axis=0)[None], jnp.min(shifted, axis=0)[None]
    flag = (2.0 * bound + hi - lo < MAX_SPAN).astype(f32)
    return jnp.stack([jnp.broadcast_to(bound + hi, flag.shape), flag], axis=-1).reshape(DEPTH * H, 2)


def _attn_prompt_call(p, bias, fast, lam_qk, qk_norm2, subln_w, layer, lam_init, kv_prev):
    nq = SEQ // ATT_TQ
    vec = lambda r: pl.BlockSpec((None, None, 1, HD), lambda b, h, i: (layer, r, 0, 0))
    head_seq = lambda n: pl.BlockSpec((n, SEQ, HD), lambda b, h, i: (0, b, h))
    stacked = jax.ShapeDtypeStruct((layer + 1, P_TOK, D_ATT), f32)
    return pl.pallas_call(
        functools.partial(_attn_prompt_kernel, layer=layer, lam_init=lam_init),
        grid=(BATCH, H, nq),
        in_specs=[
            pl.BlockSpec(memory_space=pltpu.SMEM),
            pl.BlockSpec((None, 4, HALF), lambda b, h, i: (layer, 0, 0)),
            vec(0), vec(1),
            pl.BlockSpec((None, 1, HD), lambda b, h, i: (layer, 0, 0)),
            pl.BlockSpec((None, 2, 2 * ATT_TQ, ATT_TQ), lambda b, h, i: (h, 0, 0, 0)),
            pl.BlockSpec((ATT_TQ, HD), lambda b, h, i: (b * nq + i, h)),
            pl.BlockSpec((SEQ, HD), lambda b, h, i: (b, H + h)),
            pl.BlockSpec((SEQ, HD), lambda b, h, i: (b, 2 * H + h)),
        ] + [head_seq(layer)] * len(kv_prev),
        out_specs=[pl.BlockSpec((ATT_TQ, HD), lambda b, h, i: (b * nq + i, h)), head_seq(layer + 1), head_seq(layer + 1)],
        out_shape=[jax.ShapeDtypeStruct((P_TOK, D_ATT), bf16), stacked, stacked],
        scratch_shapes=[pltpu.VMEM((SEQ, HD), bf16), pltpu.VMEM((SEQ // ATT_TK, HD, ATT_TK), bf16)],
        compiler_params=_params(("arbitrary", "arbitrary", "arbitrary"), 56),
        name="diff_attn_prompt",
    )(fast, lam_qk, qk_norm2, qk_norm2, subln_w.reshape(DEPTH, 1, HD), bias, p, p, p, *kv_prev)


def _attn_sample_kernel(lam_ref, qw_ref, kw_ref, sw_ref, bias_ref, q_ref, kx_ref, vx_ref, ck_ref, cv_ref, *rest,
                        layer, lam_init):
    o_ref, knew_ref, vnew_ref = rest[-3:]
    bd = _blockdiag_half()
    n_blk = PAST // SATT_TK
    heads, blocks = range(H), range(n_blk)
    cols = [slice(h * HD, (h + 1) * HD) for h in heads]
    if layer:
        knew_ref[:layer] = rest[0][...]
        vnew_ref[:layer] = rest[1][...]
    vnew_ref[layer] = vx_ref[...]
    q2 = [_two_map_queries(_half_rms(q_ref[:, cols[h]], qw_ref[...], bd) * (HALF ** -0.5)) for h in heads]
    kn = [_half_rms(kx_ref[:, cols[h]], kw_ref[...], bd) for h in heads]
    for h in heads:
        knew_ref[layer, :, cols[h]] = kn[h]

    def cached(ref, h, i):
        return ref[pl.ds(i * SATT_TK * H + h, SATT_TK, stride=H), :].astype(bf16)

    ss = [[_dot_nt(q2[h], cached(ck_ref, h, i)) for i in blocks] for h in heads]
    for h in heads:
        ss[h][-1] = _add_map_bias(ss[h][-1], bias_ref[h, :, :SATT_TK])
        ss[h].append(_add_map_bias(_dot_nt(q2[h], kn[h].astype(bf16)), bias_ref[h, :, SATT_TK:SATT_TK + DEC_SEQ]))
    for h in heads:
        m = functools.reduce(jnp.maximum, [jnp.max(s, axis=-1, keepdims=True) for s in ss[h]])
        ps = [jnp.exp(s - m) for s in ss[h]]
        l = functools.reduce(jnp.add, [jnp.sum(p, axis=-1, keepdims=True) for p in ps])
        vals = [cached(cv_ref, h, i) for i in blocks] + [vx_ref[:, cols[h]].astype(bf16)]
        acc = functools.reduce(jnp.add, [_dot(p.astype(bf16), v) for p, v in zip(ps, vals)])
        o_ref[:, cols[h]] = _diff_combine(l, acc, lam_ref, sw_ref[...], lam_init).astype(bf16)


def _attn_sample_call(p, bias, cache_k, cache_v, lam_qk, qk_norm2, subln_w, layer, lam_init, kv_prev):
    vec = lambda r: pl.BlockSpec((None, None, 1, HD), lambda b: (layer, r, 0, 0))
    sec = lambda c: pl.BlockSpec((DEC_SEQ, D_ATT), lambda b: (b, c))
    cache = pl.BlockSpec((None, None, PAST * H, HD), lambda b: (layer, b, 0, 0))
    stream = pl.BlockSpec((DEC_SEQ, D_ATT), lambda b: (b, 0))
    layers = lambda n: pl.BlockSpec((n, DEC_SEQ, D_ATT), lambda b: (0, b, 0))
    stacked = jax.ShapeDtypeStruct((layer + 1, S_TOK, D_ATT), f32)
    return pl.pallas_call(
        functools.partial(_attn_sample_kernel, layer=layer, lam_init=lam_init),
        grid=(DEC_BATCH,),
        in_specs=[
            pl.BlockSpec((None, 4, HALF), lambda b: (layer, 0, 0)),
            vec(0), vec(1),
            pl.BlockSpec((None, 1, HD), lambda b: (layer, 0, 0)),
            pl.BlockSpec((H, DEC_SEQ, SATT_TK + HD), lambda b: (0, 0, 0)),
            sec(0), sec(1), sec(2), cache, cache,
        ] + [layers(layer)] * len(kv_prev),
        out_specs=[stream, layers(layer + 1), layers(layer + 1)],
        out_shape=[jax.ShapeDtypeStruct((S_TOK, D_ATT), bf16), stacked, stacked],
        compiler_params=_params(("arbitrary",), 60),
        name="diff_attn_sample",
    )(lam_qk, qk_norm2, qk_norm2, subln_w.reshape(DEPTH, 1, HD), bias, p, p, p, cache_k, cache_v, *kv_prev)


def _gdn_kernel(p_ref, z_ref, ab_ref, hist_ref, s0_ref, cw_ref, alog_ref, dtb_ref, gw_ref, *rest, n_seq, layer):
    o_ref, sfin_ref, cnew_ref, xp_scr, s_scr = rest[-5:]
    per_seq = GDN_SLOTS // n_seq
    rows_seq = per_seq * CHUNK

    @pl.when(pl.program_id(1) == 0)
    def _():
        xp_scr[:, 0:8, :] = jnp.zeros((n_seq, 8, CONV_CH), f32)
        xp_scr[:, 8:16, :] = hist_ref[...]
        s_scr[...] = s0_ref[...]
        if layer:
            sfin_ref[:layer] = rest[0][...]
            cnew_ref[:layer] = rest[1][...]

    ys = []
    for sq in range(n_seq):
        xp_scr[sq, 16:16 + rows_seq, :] = p_ref[sq * rows_seq:(sq + 1) * rows_seq, :]
        x_cur = xp_scr[sq, 8:16 + rows_seq, :]
        x_prev = xp_scr[sq, 7:15 + rows_seq, :]
        pair_lo = x_cur * cw_ref[1:2, :] + x_prev * cw_ref[0:1, :]
        pair_hi = x_cur[8:] * cw_ref[3:4, :] + x_prev[8:] * cw_ref[2:3, :]
        y = _silu(pair_hi + pair_lo[6:6 + rows_seq])
        tail = xp_scr[sq, 8 + rows_seq:16 + rows_seq, :]
        xp_scr[sq, 8:16, :] = tail
        cnew_ref[layer, sq] = tail[8 - (CONV_W - 1):]
        ys += [y[c * CHUNK:(c + 1) * CHUNK] for c in range(per_seq)]

    ii = lax.broadcasted_iota(jnp.int32, (CHUNK, CHUNK), 0)
    jj = lax.broadcasted_iota(jnp.int32, (CHUNK, CHUNK), 1)
    causal = ii >= jj
    strict = ii > jj
    eye = jnp.where(ii == jj, 1.0, 0.0)
    ones = jnp.ones((HD, HD), bf16)

    units = [(n, h) for n in range(GDN_SLOTS) for h in range(H)]
    q16, k16, kb16, rhs16, gamma, q_dec, k_dec, decay = {}, {}, {}, {}, {}, {}, {}, {}
    for n in range(GDN_SLOTS):
        y = ys[n]
        ab = ab_ref[n * CHUNK:(n + 1) * CHUNK, :]
        pre = ab + dtb_ref[...]
        softplus = jnp.maximum(pre, 0.0) + jnp.log1p(jnp.exp(-jnp.abs(pre)))
        g_all = -jnp.exp(alog_ref[...]) * softplus
        beta_all = jax.nn.sigmoid(ab)
        for h in range(H):
            u = (n, h)
            qx = y[:, h * HD:(h + 1) * HD]
            kx = y[:, D_GDN + h * HD:D_GDN + (h + 1) * HD]
            v = y[:, 2 * D_GDN + h * HD:2 * D_GDN + (h + 1) * HD]
            q = qx * lax.rsqrt(_dot((qx * qx).astype(bf16), ones) + EPS) * (HD ** -0.5)
            k = kx * lax.rsqrt(_dot((kx * kx).astype(bf16), ones) + EPS)
            g = g_all[:, h:h + 1]
            beta = beta_all[:, H + h:H + h + 1]
            gc_row = jnp.sum(jnp.where(ii <= jj, g, 0.0), axis=0, keepdims=True)
            gc = jnp.sum(jnp.where(ii == jj, gc_row, 0.0), axis=1, keepdims=True)
            gc_last = jnp.sum(g, axis=0, keepdims=True)
            gamma[u] = jnp.where(causal, jnp.exp(jnp.where(causal, gc - gc_row, 0.0)), 0.0)
            egc = jnp.exp(gc)
            kb = k * beta
            q16[u], k16[u], kb16[u] = q.astype(bf16), k.astype(bf16), kb.astype(bf16)
            rhs16[u] = jnp.concatenate([kb * egc, v * beta], axis=-1).astype(bf16)
            q_dec[u] = q * egc
            k_dec[u] = k * jnp.exp(gc_last - gc)
            decay[u] = jnp.exp(gc_last)

    kk = {u: _dot_nt(kb16[u], k16[u]) for u in units}
    qk = {u: _dot_nt(q16[u], k16[u]) for u in units}
    a = {u: jnp.where(strict, kk[u] * gamma[u], 0.0) for u in units}
    attn = {u: qk[u] * gamma[u] for u in units}

    base = 8
    d = {u: jnp.where(ii // base == jj // base, a[u], 0.0) for u in units}
    t = {u: eye - d[u] for u in units}
    d16 = {u: d[u].astype(bf16) for u in units}
    pw16 = {u: _dot(d16[u], d16[u]).astype(bf16) for u in units}
    t = {u: t[u] + _dot(t[u].astype(bf16), pw16[u]) for u in units}
    pw16 = {u: _dot(pw16[u], pw16[u]).astype(bf16) for u in units}
    t = {u: t[u] + _dot(t[u].astype(bf16), pw16[u]) for u in units}
    size = base
    while size < CHUNK:
        off = (ii // (2 * size) == jj // (2 * size)) & (ii // size != jj // size)
        t16 = {u: t[u].astype(bf16) for u in units}
        ta = {u: _dot(t16[u], jnp.where(off, a[u], 0.0).astype(bf16)) for u in units}
        t = {u: t[u] - _dot(ta[u].astype(bf16), t16[u]) for u in units}
        size *= 2
    sol = {u: _dot(t[u].astype(bf16), rhs16[u]) for u in units}

    s_cur = {}
    for n in range(GDN_SLOTS):
        sq = n // per_seq
        if n % per_seq == 0:
            s_cur = {h: s_scr[sq, h] for h in range(H)}
        ws = {h: _dot(jnp.concatenate([sol[(n, h)][:, :HD], q_dec[(n, h)]], axis=0).astype(bf16), s_cur[h].astype(bf16))
              for h in range(H)}
        vn16 = {h: (sol[(n, h)][:, HD:] - ws[h][:CHUNK]).astype(bf16) for h in range(H)}
        r = {h: _dot(jnp.concatenate([attn[(n, h)], k_dec[(n, h)].T], axis=0).astype(bf16), vn16[h]) for h in range(H)}
        for h in range(H):
            cols = slice(h * HD, (h + 1) * HD)
            rows = slice(n * CHUNK, (n + 1) * CHUNK)
            s_cur[h] = s_cur[h] * decay[(n, h)] + r[h][CHUNK:]
            o = ws[h][CHUNK:] + r[h][:CHUNK]
            o = o * lax.rsqrt(jnp.mean(o * o, axis=-1, keepdims=True) + EPS) * gw_ref[...]
            o_ref[rows, cols] = (o * _silu(z_ref[rows, cols])).astype(bf16)
        if n % per_seq == per_seq - 1:
            for h in range(H):
                s_scr[sq, h] = s_cur[h]

    @pl.when(pl.program_id(1) == pl.num_programs(1) - 1)
    def _():
        sfin_ref[layer] = s_scr[...]


def _gdn_call(p, hist, s0, conv_w, a_log, dt_bias, gdn_norm_w, layer, n_blocks, steps, n_seq, prev):
    br = GDN_SLOTS * CHUNK
    rows = lambda col, width: pl.BlockSpec((br, width), lambda b, c: (b * steps + c, col))
    vec = pl.BlockSpec((None, 1, HD), lambda b, c: (layer, 0, 0))
    state = lambda n: pl.BlockSpec((n, n_seq, H, HD, HD), lambda b, c: (0, b, 0, 0, 0))
    tail = lambda n: pl.BlockSpec((n, n_seq, CONV_W - 1, CONV_CH), lambda b, c: (0, b, 0, 0))
    return pl.pallas_call(
        functools.partial(_gdn_kernel, n_seq=n_seq, layer=layer),
        grid=(n_blocks, steps),
        in_specs=[
            rows(COL_CONV // CONV_CH, CONV_CH),
            rows(COL_Z // D_GDN, D_GDN),
            rows(COL_AB // HD, HD),
            pl.BlockSpec((n_seq, 8, CONV_CH), lambda b, c: (b, 0, 0)),
            pl.BlockSpec((n_seq, H, HD, HD), lambda b, c: (b, 0, 0, 0)),
            pl.BlockSpec((None, CONV_W, CONV_CH), lambda b, c: (layer, 0, 0)),
            vec, vec, vec,
        ] + ([state(layer), tail(layer)] if prev else []),
        out_specs=[pl.BlockSpec((br, D_GDN), lambda b, c: (b * steps + c, 0)), state(layer + 1), tail(layer + 1)],
        out_shape=[
            jax.ShapeDtypeStruct((n_blocks * steps * br, D_GDN), bf16),
            jax.ShapeDtypeStruct((layer + 1, n_blocks * n_seq, H, HD, HD), f32),
            jax.ShapeDtypeStruct((layer + 1, n_blocks * n_seq, CONV_W - 1, CONV_CH), f32),
        ],
        scratch_shapes=[pltpu.VMEM((n_seq, 16 + br // n_seq, CONV_CH), f32), pltpu.VMEM((n_seq, H, HD, HD), f32)],
        compiler_params=_params(("arbitrary", "arbitrary"), 48),
        name="gated_deltanet",
    )(p, p, p, hist, s0, conv_w, a_log, dt_bias, gdn_norm_w, *prev)


def _pad_lanes(v):
    return jnp.pad(v, ((0, 0), (0, HD - v.shape[-1]))).reshape(DEPTH, 1, HD)


def kernel(x_prompt, x_sample, cache_k, cache_v, state_gdn, state_conv, c_prompt, c_sample, w_mod, b_mod, norm_g, ffn_up, ffn_down, w_in, w_out, qk_norm, lam_qk, subln_w, rpb_table, conv_w, a_log, dt_bias, gdn_norm_w):
    xp = x_prompt.reshape(P_TOK, D)
    xs = x_sample.reshape(S_TOK, D)
    c_all = jnp.concatenate([c_prompt, c_sample, jnp.zeros((N_CBATCH - BATCH - DEC_BATCH, D), f32)], axis=0)

    w_up16 = ffn_up.astype(bf16)
    w_down16 = ffn_down.astype(bf16)
    w_in16 = jnp.pad(w_in.astype(bf16), ((0, 0), (0, 0), (0, N_PROJ - N_IN)))
    w_out16 = w_out.astype(bf16)
    norm_g4 = norm_g.reshape(DEPTH, 3, 1, D)
    qk_norm2 = jnp.concatenate([qk_norm, qk_norm], axis=-1).reshape(DEPTH, 2, 1, HD)
    gdn_w = gdn_norm_w.reshape(DEPTH, 1, HD)
    a_log_p, dt_bias_p = _pad_lanes(a_log), _pad_lanes(dt_bias)
    ck = cache_k.reshape(DEPTH, DEC_BATCH, PAST * H, HD)
    cv = cache_v.reshape(DEPTH, DEC_BATCH, PAST * H, HD)
    hist_s = jnp.pad(state_conv, ((0, 0), (0, 0), (8 - (CONV_W - 1), 0), (0, 0)))
    hist_p = jnp.zeros((BATCH, 8, CONV_CH), f32)
    s0_p = jnp.zeros((BATCH, H, HD, HD), f32)

    mod = _mod_call(c_all, w_mod, b_mod).reshape(DEPTH, N_CBATCH, N_MOD, D)
    modg_p = jnp.transpose(jnp.repeat(mod[:, :BATCH], SEQ // GROUP, axis=1), (0, 2, 1, 3))
    modg_s = jnp.transpose(mod[:, BATCH:BATCH + DEC_BATCH], (0, 2, 1, 3))

    pbias, sbias = _bias_call(rpb_table)
    fast = _softmax_reference(qk_norm, rpb_table)

    gdn_rows = GDN_SLOTS * CHUNK
    kv_p, kv_s = (), ()
    sc_p, sc_s = (), ()
    for l in range(DEPTH):
        lam_init = 0.8 - 0.6 * math.exp(-0.3 * l)
        xp = _ffn_call(xp, modg_p[l], norm_g4, w_up16, w_down16, l, 0, 0)
        xs = _ffn_call(xs, modg_s[l], norm_g4, w_up16, w_down16, l, 0, 0)
        pp = _proj_call(xp, modg_p[l], norm_g4, w_in16, l)
        ps = _proj_call(xs, modg_s[l], norm_g4, w_in16, l)

        oa_p, *kv_p = _attn_prompt_call(pp, pbias, fast, lam_qk, qk_norm2, subln_w, l, lam_init, kv_p)
        oa_s, *kv_s = _attn_sample_call(ps, sbias, ck, cv, lam_qk, qk_norm2, subln_w, l, lam_init, kv_s)
        og_p, *sc_p = _gdn_call(pp, hist_p, s0_p, conv_w, a_log_p, dt_bias_p, gdn_w, l,
                                BATCH, SEQ // gdn_rows, 1, sc_p)
        og_s, *sc_s = _gdn_call(ps, hist_s[l], state_gdn[l], conv_w, a_log_p, dt_bias_p, gdn_w, l,
                                DEC_BATCH // GDN_SLOTS, 1, GDN_SLOTS, sc_s)

        xp = _out_call(xp, modg_p[l], oa_p, og_p, w_out16, l)
        xs = _out_call(xs, modg_s[l], oa_s, og_s, w_out16, l)
        xp = _ffn_call(xp, modg_p[l], norm_g4, w_up16, w_down16, l, 1, 2)
        xs = _ffn_call(xs, modg_s[l], norm_g4, w_up16, w_down16, l, 1, 2)

    return (xp.reshape(BATCH, SEQ, D), xs.reshape(DEC_BATCH, DEC_SEQ, D),
            kv_p[0].reshape(DEPTH, BATCH, SEQ, H, HD), kv_p[1].reshape(DEPTH, BATCH, SEQ, H, HD), sc_p[0], sc_p[1],
            kv_s[0].reshape(DEPTH, DEC_BATCH, DEC_SEQ, H, HD), kv_s[1].reshape(DEPTH, DEC_BATCH, DEC_SEQ, H, HD),
            sc_s[0], sc_s[1])
```
